```python
import jax, jax.numpy as jnp
from jax import lax
import numpy as np

D_MODEL = 1024
BATCH = 2
SEQ = 8192
DEPTH = 1

GRID_W = 64
N_MEM = 256
D_MIX = D_MODEL
CONV_CH = D_MIX // 2
CONV_GROUPS = 8
CONV_K = 31
HEAD_DIM = 64
ATT_HEADS = (D_MIX - CONV_CH) // HEAD_DIM
ATT_KV_HEADS = 2
ATT_WIDTH = ATT_HEADS * HEAD_DIM
KV_WIDTH = ATT_KV_HEADS * HEAD_DIM
IN_COLS = 2 * CONV_CH + ATT_WIDTH + 2 * KV_WIDTH
ROPE_THETA = 10000.0
Q_BLOCK = 128
MEM_HEADS = 4
MEM_HEAD_DIM = D_MODEL // MEM_HEADS
D_FF = 2816
FFN_CONV_K = 3
EPS = 1e-6

kernel_name = "hybrid_conv_gqa_axial_encoder_block"


def rmsnorm(x, g):
    xf = x.astype(jnp.float32)
    y = xf * lax.rsqrt(jnp.mean(xf * xf, axis=-1, keepdims=True) + EPS)
    return (y * g.astype(jnp.float32)).astype(x.dtype)


def layernorm(x, g, b):
    xf = x.astype(jnp.float32)
    mu = jnp.mean(xf, axis=-1, keepdims=True)
    var = jnp.mean(jnp.square(xf - mu), axis=-1, keepdims=True)
    y = (xf - mu) * lax.rsqrt(var + EPS)
    return (y * g.astype(jnp.float32) + b.astype(jnp.float32)).astype(x.dtype)


def dwconv_centred(x, w, b):
    C = x.shape[-1]
    K = w.shape[0]
    pad = (K - 1) // 2
    y = lax.conv_general_dilated(
        x, w[:, None, :].astype(x.dtype), window_strides=(1,), padding=[(pad, pad)],
        dimension_numbers=("NWC", "WIO", "NWC"), feature_group_count=C)
    return y + b.astype(x.dtype)


def axial_rope_tables(S):
    rows = S // GRID_W
    r = jnp.repeat(jnp.arange(rows, dtype=jnp.float32), GRID_W)
    c = jnp.tile(jnp.arange(GRID_W, dtype=jnp.float32), rows)
    axis_dim = HEAD_DIM // 2
    inv = ROPE_THETA ** (-jnp.arange(0, axis_dim, 2, dtype=jnp.float32) / axis_dim)
    ang_r = r[:, None] * inv[None, :]
    ang_c = c[:, None] * inv[None, :]
    return jnp.cos(ang_r), jnp.sin(ang_r), jnp.cos(ang_c), jnp.sin(ang_c)


def rotate_half(x, cos, sin):
    x1, x2 = jnp.split(x, 2, axis=-1)
    return jnp.concatenate([x1 * cos - x2 * sin, x2 * cos + x1 * sin], axis=-1)


def apply_axial_rope(x, tabs):
    cr, sr, cc, sc = tabs
    xf = x.astype(jnp.float32)
    xr, xc = jnp.split(xf, 2, axis=-1)
    out = jnp.concatenate([rotate_half(xr, cr, sr), rotate_half(xc, cc, sc)], axis=-1)
    return out.astype(x.dtype)


def gqa_blocked(q, k, v):
    B, H, S, d = q.shape
    G = H // ATT_KV_HEADS
    nb = S // Q_BLOCK
    qb = q.reshape(B, ATT_KV_HEADS, G, nb, Q_BLOCK, d).transpose(3, 0, 1, 2, 4, 5)
    scale = d ** -0.5

    def one_block(qblk):
        s = jnp.einsum("bkgqd,bksd->bkgqs", qblk, k).astype(jnp.float32) * scale
        p = jax.nn.softmax(s, axis=-1).astype(v.dtype)
        return jnp.einsum("bkgqs,bksd->bkgqd", p, v)

    o = lax.map(one_block, qb)
    return o.transpose(1, 0, 4, 2, 3, 5).reshape(B, S, H * d)


def parallel_mixer(h, w_in, conv_dw, conv_dw_b, conv_ln_g, conv_ln_b, q_norm_g, k_norm_g, w_out, tabs):
    B, S, _ = h.shape
    z = jnp.einsum("bsd,dc->bsc", h, w_in)
    o0 = 2 * CONV_CH
    a_val, a_gate = z[..., :CONV_CH], z[..., CONV_CH:o0]
    q = z[..., o0:o0 + ATT_WIDTH]
    k = z[..., o0 + ATT_WIDTH:o0 + ATT_WIDTH + KV_WIDTH]
    v = z[..., o0 + ATT_WIDTH + KV_WIDTH:]

    a = a_val * jax.nn.sigmoid(a_gate)
    a = dwconv_centred(a, conv_dw, conv_dw_b)
    a = layernorm(a, conv_ln_g, conv_ln_b)
    a = jax.nn.silu(a)

    q = q.reshape(B, S, ATT_HEADS, HEAD_DIM).transpose(0, 2, 1, 3)
    k = k.reshape(B, S, ATT_KV_HEADS, HEAD_DIM).transpose(0, 2, 1, 3)
    v = v.reshape(B, S, ATT_KV_HEADS, HEAD_DIM).transpose(0, 2, 1, 3)
    q = apply_axial_rope(rmsnorm(q, q_norm_g), tabs)
    k = apply_axial_rope(rmsnorm(k, k_norm_g), tabs)
    att = gqa_blocked(q, k, v)

    mixed = jnp.concatenate([a, att], axis=-1)
    return jnp.einsum("bsc,cd->bsd", mixed, w_out)


def memory_cross_attention(h, mem_n, w_mem_q, w_mem_kv, w_mem_o):
    B, S, _ = h.shape
    M = mem_n.shape[1]
    q = jnp.einsum("bsd,de->bse", h, w_mem_q).reshape(B, S, MEM_HEADS, MEM_HEAD_DIM)
    kv = jnp.einsum("bmd,de->bme", mem_n, w_mem_kv)
    k = kv[..., :D_MODEL].reshape(B, M, MEM_HEADS, MEM_HEAD_DIM)
    v = kv[..., D_MODEL:].reshape(B, M, MEM_HEADS, MEM_HEAD_DIM)
    s = jnp.einsum("bshd,bmhd->bhsm", q, k).astype(jnp.float32) * (MEM_HEAD_DIM ** -0.5)
    p = jax.nn.softmax(s, axis=-1).astype(v.dtype)
    o = jnp.einsum("bhsm,bmhd->bshd", p, v).reshape(B, S, D_MODEL)
    return jnp.einsum("bse,ed->bsd", o, w_mem_o)


def conv_gated_ffn(h, w_up, ffn_dw, ffn_dw_b, w_down):
    u = jnp.einsum("bsd,df->bsf", h, w_up)
    u = dwconv_centred(u, ffn_dw, ffn_dw_b)
    gate, val = u[..., :D_FF], u[..., D_FF:]
    return jnp.einsum("bsf,fd->bsd", jax.nn.gelu(gate, approximate=True) * val, w_down)


def setup_inputs(seed: int = 0) -> dict:
    key = jax.random.key(seed)
    ks = iter(jax.random.split(key, 32))
    f32 = jnp.float32

    def nrm(shape, scale):
        return jax.random.normal(next(ks), shape, f32) * scale

    def gain(shape):
        return 1.0 + 0.02 * jax.random.normal(next(ks), shape, f32)

    L = DEPTH
    return {
        "x": jax.random.normal(next(ks), (BATCH, SEQ, D_MODEL), f32),
        "mem": jax.random.normal(next(ks), (BATCH, N_MEM, D_MODEL), f32),
        "norm_mix_pre": gain((L, D_MODEL)),
        "w_in": nrm((L, D_MODEL, IN_COLS), D_MODEL ** -0.5),
        "conv_dw": nrm((L, CONV_K, CONV_CH), CONV_K ** -0.5),
        "conv_dw_b": nrm((L, CONV_CH), 0.02),
        "conv_ln_g": gain((L, CONV_CH)),
        "conv_ln_b": nrm((L, CONV_CH), 0.02),
        "q_norm_g": gain((L, HEAD_DIM)),
        "k_norm_g": gain((L, HEAD_DIM)),
        "w_out": nrm((L, D_MIX, D_MODEL), D_MIX ** -0.5),
        "norm_mix_post": gain((L, D_MODEL)),
        "norm_mem_pre": gain((L, D_MODEL)),
        "mem_norm_g": gain((L, D_MODEL)),
        "w_mem_q": nrm((L, D_MODEL, D_MODEL), D_MODEL ** -0.5),
        "w_mem_kv": nrm((L, D_MODEL, 2 * D_MODEL), D_MODEL ** -0.5),
        "w_mem_o": nrm((L, D_MODEL, D_MODEL), D_MODEL ** -0.5),
        "norm_mem_post": gain((L, D_MODEL)),
        "norm_ffn_pre": gain((L, D_MODEL)),
        "w_up": nrm((L, D_MODEL, 2 * D_FF), D_MODEL ** -0.5),
        "ffn_dw": nrm((L, FFN_CONV_K, 2 * D_FF), FFN_CONV_K ** -0.5),
        "ffn_dw_b": nrm((L, 2 * D_FF), 0.02),
        "w_down": nrm((L, D_FF, D_MODEL), D_FF ** -0.5),
        "norm_ffn_post": gain((L, D_MODEL)),
    }


def reference(x, mem, norm_mix_pre, w_in, conv_dw, conv_dw_b, conv_ln_g, conv_ln_b,
              q_norm_g, k_norm_g, w_out, norm_mix_post, norm_mem_pre, mem_norm_g,
              w_mem_q, w_mem_kv, w_mem_o, norm_mem_post, norm_ffn_pre, w_up, ffn_dw,
              ffn_dw_b, w_down, norm_ffn_post):
    S = x.shape[1]
    tabs = axial_rope_tables(S)
    for l in range(DEPTH):
        h = rmsnorm(x, norm_mix_pre[l])
        y = parallel_mixer(h, w_in[l], conv_dw[l], conv_dw_b[l], conv_ln_g[l], conv_ln_b[l],
                           q_norm_g[l], k_norm_g[l], w_out[l], tabs)
        x = x + rmsnorm(y, norm_mix_post[l])
        h = rmsnorm(x, norm_mem_pre[l])
        mem_n = rmsnorm(mem, mem_norm_g[l])
        y = memory_cross_attention(h, mem_n, w_mem_q[l], w_mem_kv[l], w_mem_o[l])
        x = x + rmsnorm(y, norm_mem_post[l])
        h = rmsnorm(x, norm_ffn_pre[l])
        y = conv_gated_ffn(h, w_up[l], ffn_dw[l], ffn_dw_b[l], w_down[l])
        x = x + rmsnorm(y, norm_ffn_post[l])
    return x
```

```python
import functools
import math

import jax
import jax.numpy as jnp
from jax import lax
from jax.experimental import pallas as pl
from jax.experimental.pallas import tpu as pltpu

F32 = jnp.float32
BF16 = jnp.bfloat16

GRID_W = 64
HEAD_DIM = 64
ATT_HEADS = 8
ATT_KV_HEADS = 2
HEADS_PER_KV = ATT_HEADS // ATT_KV_HEADS
CONV_CH = 512
CONV_K = 31
CONV_PAD = (CONV_K - 1) // 2
MEM_HEADS = 4
MEM_HEAD_DIM = 256
D_FF = 2816
ROPE_THETA = 10000.0
EPS = 1e-6
LOG2E = 1.4426950408889634

LANES = 128
SUBLANES = 8
BF16_ROWS = 16
VMEM_LIMIT = 56 * 1024 * 1024

TOK_TILE = 512
ATT_TQ = 256
ATT_TK = 512
V_ROWS = HEAD_DIM + BF16_ROWS
CONV_HALO = 16
FFN_HALO = 8
FFN_CHUNK = 256


def _rms(xf, g):
    return xf * lax.rsqrt(jnp.mean(xf * xf, axis=-1, keepdims=True) + EPS) * g


def _dot(a, b):
    return jnp.dot(a, b, preferred_element_type=F32)


def _resident(shape):
    zeros = (0,) * len(shape)
    return pl.BlockSpec(shape, lambda *_: zeros, pipeline_mode=pl.Buffered(1))


def _mem_kv_kernel(mem_ref, g_ref, w_ref, kt_ref, v_ref):
    d = mem_ref.shape[-1]
    mn = _rms(mem_ref[0], g_ref[...]).astype(BF16)
    kv = _dot(mn, w_ref[...])
    kt_ref[0] = (kv[:, :d] * (MEM_HEAD_DIM ** -0.5)).T.astype(BF16)
    v_ref[0] = kv[:, d:].astype(BF16)


def _mem_kv(mem, g, w_kv):
    b, m, d = mem.shape
    return pl.pallas_call(
        _mem_kv_kernel,
        grid=(b,),
        in_specs=[
            pl.BlockSpec((1, m, d), lambda i: (i, 0, 0)),
            _resident((1, d)),
            _resident((d, 2 * d)),
        ],
        out_specs=[
            pl.BlockSpec((1, d, m), lambda i: (i, 0, 0)),
            pl.BlockSpec((1, m, d), lambda i: (i, 0, 0)),
        ],
        out_shape=[
            jax.ShapeDtypeStruct((b, d, m), BF16),
            jax.ShapeDtypeStruct((b, m, d), BF16),
        ],
        compiler_params=pltpu.CompilerParams(
            dimension_semantics=("arbitrary",), vmem_limit_bytes=VMEM_LIMIT),
        name="mem_kv",
    )(mem, g, w_kv)


def _rope_partner(t):
    q = HEAD_DIM // 4
    return jnp.concatenate([t[q:2 * q], t[:q], t[3 * q:], t[2 * q:3 * q]], axis=0)


def _norm_rope_t(zt, g, cos_t, sin_t):
    r = lax.rsqrt(jnp.mean(zt * zt, axis=0, keepdims=True) + EPS)
    y = zt * r * g
    return y * cos_t + _rope_partner(y) * sin_t


def _in_proj_kernel(x_ref, g_ref, wa_ref, wt_ref, gq_ref, gk_ref, cos_ref, sin_ref,
                    a_ref, qt_ref, k_ref, vt_ref):
    t = x_ref.shape[1]
    h = _rms(x_ref[0], g_ref[...]).astype(BF16)
    za = _dot(h, wa_ref[...])
    a_ref[0] = za[:, :CONV_CH] * jax.nn.sigmoid(za[:, CONV_CH:])
    zt = lax.dot_general(wt_ref[...], h, (((1,), (1,)), ((), ())),
                         preferred_element_type=F32)
    reps = t // LANES
    gq = jnp.concatenate([gq_ref[...]] * reps, axis=1)
    gk = jnp.concatenate([gk_ref[...]] * reps, axis=1)
    cos_t = cos_ref[...]
    sin_t = sin_ref[...]
    qscale = (HEAD_DIM ** -0.5) * LOG2E
    zero = jnp.zeros((HEAD_DIM, t), BF16)
    for hd in range(ATT_HEADS):
        q = _norm_rope_t(zt[hd * HEAD_DIM:(hd + 1) * HEAD_DIM], gq, cos_t, sin_t)
        q = (q * qscale).astype(BF16)
        kv = hd // HEADS_PER_KV
        for j in range(ATT_KV_HEADS):
            qt_ref[0, hd, j * HEAD_DIM:(j + 1) * HEAD_DIM, :] = q if j == kv else zero
    o = ATT_HEADS * HEAD_DIM
    ks = [_norm_rope_t(zt[o + j * HEAD_DIM:o + (j + 1) * HEAD_DIM], gk, cos_t, sin_t)
          for j in range(ATT_KV_HEADS)]
    k_ref[0] = jnp.concatenate(ks, axis=0).T.astype(BF16)
    o += ATT_KV_HEADS * HEAD_DIM
    row = lax.broadcasted_iota(jnp.int32, (BF16_ROWS, ATT_TK), 0)
    ones_rows = jnp.where(row == 0, 1.0, 0.0).astype(BF16)
    for j in range(ATT_KV_HEADS):
        v = zt[o + j * HEAD_DIM:o + (j + 1) * HEAD_DIM].astype(BF16)
        for c in range(t // ATT_TK):
            vt_ref[0, j, c, :HEAD_DIM, :] = v[:, c * ATT_TK:(c + 1) * ATT_TK]
            vt_ref[0, j, c, HEAD_DIM:, :] = ones_rows


def _in_proj(x, g, w_a, w_t, gq, gk, cos_t, sin_t):
    b, s, d = x.shape
    t = TOK_TILE
    nt = s // t
    qkv_rows = w_t.shape[0]
    return pl.pallas_call(
        _in_proj_kernel,
        grid=(b, nt),
        in_specs=[
            pl.BlockSpec((1, t, d), lambda i, j: (i, j, 0)),
            _resident((1, d)),
            _resident((d, 2 * CONV_CH)),
            _resident((qkv_rows, d)),
            _resident((HEAD_DIM, LANES)),
            _resident((HEAD_DIM, LANES)),
            pl.BlockSpec((HEAD_DIM, t), lambda i, j: (0, j)),
            pl.BlockSpec((HEAD_DIM, t), lambda i, j: (0, j)),
        ],
        out_specs=[
            pl.BlockSpec((1, t, CONV_CH), lambda i, j: (i, j, 0)),
            pl.BlockSpec((1, ATT_HEADS, 2 * HEAD_DIM, t), lambda i, j: (i, 0, 0, j)),
            pl.BlockSpec((1, t, 2 * HEAD_DIM), lambda i, j: (i, j, 0)),
            pl.BlockSpec((1, ATT_KV_HEADS, t // ATT_TK, V_ROWS, ATT_TK),
                         lambda i, j: (i, 0, j, 0, 0)),
        ],
        out_shape=[
            jax.ShapeDtypeStruct((b, s, CONV_CH), F32),
            jax.ShapeDtypeStruct((b, ATT_HEADS, 2 * HEAD_DIM, s), BF16),
            jax.ShapeDtypeStruct((b, s, 2 * HEAD_DIM), BF16),
            jax.ShapeDtypeStruct((b, ATT_KV_HEADS, s // ATT_TK, V_ROWS, ATT_TK), BF16),
        ],
        compiler_params=pltpu.CompilerParams(
            dimension_semantics=("arbitrary", "arbitrary"),
            vmem_limit_bytes=VMEM_LIMIT),
        name="in_proj",
    )(x, g, w_a, w_t, gq, gk, cos_t, sin_t)


def _attn_kernel(qt_ref, k_ref, vt_ref, o_ref, m_ref, acc_ref):
    nkb = vt_ref.shape[2]
    tq = qt_ref.shape[-1]
    m_ref[...] = jnp.full(m_ref.shape, -jnp.inf, F32)
    acc_ref[...] = jnp.zeros(acc_ref.shape, F32)

    def body(kb, carry):
        k_blk = k_ref[0, pl.ds(pl.multiple_of(kb * ATT_TK, ATT_TK), ATT_TK), :]
        v_blk = vt_ref[0, 0, kb]
        for hd in range(HEADS_PER_KV):
            st = _dot(k_blk, qt_ref[0, hd])
            m_old = m_ref[hd]
            m_new = jnp.maximum(m_old, jnp.max(st, axis=0, keepdims=True))
            p = jnp.exp2(st - m_new).astype(BF16)
            alpha = jnp.exp2(m_old - m_new)
            acc_ref[hd] = acc_ref[hd] * alpha + _dot(v_blk, p)
            m_ref[hd] = m_new
        return carry

    lax.fori_loop(0, nkb, body, 0)
    outs = []
    for hd in range(HEADS_PER_KV):
        acc = acc_ref[hd]
        outs.append(acc[:HEAD_DIM] / acc[HEAD_DIM:HEAD_DIM + 1])
    o_ref[0] = jnp.concatenate(outs, axis=0).T.astype(BF16)
    del tq


def _attention(qt, k, vt):
    b, _, _, s = qt.shape
    nkb = vt.shape[2]
    width = HEADS_PER_KV * HEAD_DIM
    return pl.pallas_call(
        _attn_kernel,
        grid=(b, ATT_KV_HEADS, s // ATT_TQ),
        in_specs=[
            pl.BlockSpec((1, HEADS_PER_KV, 2 * HEAD_DIM, ATT_TQ),
                         lambda i, j, q: (i, j, 0, q)),
            pl.BlockSpec((1, s, 2 * HEAD_DIM), lambda i, j, q: (i, 0, 0)),
            pl.BlockSpec((1, 1, nkb, V_ROWS, ATT_TK), lambda i, j, q: (i, j, 0, 0, 0)),
        ],
        out_specs=pl.BlockSpec((1, ATT_TQ, width), lambda i, j, q: (i, q, j)),
        out_shape=jax.ShapeDtypeStruct((b, s, ATT_KV_HEADS * width), BF16),
        scratch_shapes=[
            pltpu.VMEM((HEADS_PER_KV, 1, ATT_TQ), F32),
            pltpu.VMEM((HEADS_PER_KV, V_ROWS, ATT_TQ), F32),
        ],
        compiler_params=pltpu.CompilerParams(
            dimension_semantics=("arbitrary", "arbitrary", "arbitrary"),
            vmem_limit_bytes=VMEM_LIMIT),
        name="attention",
    )(qt, k, vt)


def _post_attn_kernel(x_ref, a_ref, ap_ref, an_ref, att_ref, dw_ref, dwb_ref,
                      lng_ref, lnb_ref, wout_ref, gpost_ref, gmem_ref, wq_ref,
                      kt_ref, v_ref, wo_ref, gmpost_ref, o_ref, abuf_ref):
    t = x_ref.shape[1]
    j = pl.program_id(1)
    nt = pl.num_programs(1)
    prev = jnp.where(j > 0, ap_ref[0], 0.0)
    nxt = jnp.where(j < nt - 1, an_ref[0], 0.0)
    abuf_ref[:CONV_HALO] = prev
    abuf_ref[CONV_HALO:CONV_HALO + t] = a_ref[0]
    abuf_ref[CONV_HALO + t:] = nxt
    rows = 64
    base = CONV_HALO - CONV_PAD
    for r0 in range(0, t, rows):
        for c0 in range(0, CONV_CH, LANES):
            acc = jnp.broadcast_to(dwb_ref[:, c0:c0 + LANES], (rows, LANES))
            for kk in range(CONV_K):
                acc = acc + (dw_ref[kk:kk + 1, c0:c0 + LANES]
                             * abuf_ref[base + r0 + kk:base + r0 + kk + rows, c0:c0 + LANES])
            abuf_ref[r0:r0 + rows, c0:c0 + LANES] = acc
    c = abuf_ref[:t]
    mu = jnp.mean(c, axis=-1, keepdims=True)
    cc = c - mu
    var = jnp.mean(cc * cc, axis=-1, keepdims=True)
    y = cc * lax.rsqrt(var + EPS) * lng_ref[...] + lnb_ref[...]
    y = y * jax.nn.sigmoid(y)
    mixed = jnp.concatenate([y.astype(BF16), att_ref[0]], axis=-1)
    x1 = x_ref[0] + _rms(_dot(mixed, wout_ref[...]), gpost_ref[...])
    h = _rms(x1, gmem_ref[...]).astype(BF16)
    q = _dot(h, wq_ref[...]).astype(BF16)
    heads = []
    for hd in range(MEM_HEADS):
        sl = slice(hd * MEM_HEAD_DIM, (hd + 1) * MEM_HEAD_DIM)
        s = _dot(q[:, sl], kt_ref[0, sl, :])
        p = jnp.exp(s - jnp.max(s, axis=-1, keepdims=True))
        l = jnp.sum(p, axis=-1, keepdims=True)
        heads.append((_dot(p.astype(BF16), v_ref[0, :, sl]) / l).astype(BF16))
    o = jnp.concatenate(heads, axis=-1)
    o_ref[0] = x1 + _rms(_dot(o, wo_ref[...]), gmpost_ref[...])


def _post_attn(x, a, att, dw, dwb, lng, lnb, wout, gpost, gmem, wq, kt, v, wo, gmpost):
    b, s, d = x.shape
    t = TOK_TILE
    nt = s // t
    hb = t // CONV_HALO
    m = v.shape[1]
    vec = _resident((1, d))
    cvec = _resident((1, CONV_CH))
    return pl.pallas_call(
        _post_attn_kernel,
        grid=(b, nt),
        in_specs=[
            pl.BlockSpec((1, t, d), lambda i, j: (i, j, 0)),
            pl.BlockSpec((1, t, CONV_CH), lambda i, j: (i, j, 0)),
            pl.BlockSpec((1, CONV_HALO, CONV_CH),
                         lambda i, j: (i, jnp.maximum(j * hb - 1, 0), 0)),
            pl.BlockSpec((1, CONV_HALO, CONV_CH),
                         lambda i, j: (i, jnp.minimum((j + 1) * hb, nt * hb - 1), 0)),
            pl.BlockSpec((1, t, CONV_CH), lambda i, j: (i, j, 0)),
            _resident((CONV_K, CONV_CH)),
            cvec, cvec, cvec,
            _resident((d, d)),
            vec, vec,
            _resident((d, d)),
            pl.BlockSpec((1, d, m), lambda i, j: (i, 0, 0)),
            pl.BlockSpec((1, m, d), lambda i, j: (i, 0, 0)),
            _resident((d, d)),
            vec,
        ],
        out_specs=pl.BlockSpec((1, t, d), lambda i, j: (i, j, 0)),
        out_shape=jax.ShapeDtypeStruct((b, s, d), F32),
        scratch_shapes=[pltpu.VMEM((t + 2 * CONV_HALO, CONV_CH), F32)],
        compiler_params=pltpu.CompilerParams(
            dimension_semantics=("arbitrary", "arbitrary"),
            vmem_limit_bytes=VMEM_LIMIT),
        name="post_attn",
    )(x, a, a, a, att, dw, dwb, lng, lnb, wout, gpost, gmem, wq, kt, v, wo, gmpost)


def _gelu_tanh(x):
    c = math.sqrt(2.0 / math.pi)
    return 0.5 * x * (1.0 + jnp.tanh(c * (x + 0.044715 * (x * x * x))))


def _ffn_kernel(x_ref, xp_ref, xn_ref, g_ref, wup_ref, dw_ref, dwb_ref, wdn_ref,
                gpost_ref, o_ref, hbuf_ref, hb_ref, acc_ref):
    t = x_ref.shape[1]
    j = pl.program_id(1)
    nt = pl.num_programs(1)
    g = g_ref[...]
    hbuf_ref[:FFN_HALO] = jnp.where(j > 0, _rms(xp_ref[0], g), 0.0)
    hbuf_ref[FFN_HALO:FFN_HALO + t] = _rms(x_ref[0], g)
    hbuf_ref[FFN_HALO + t:] = jnp.where(j < nt - 1, _rms(xn_ref[0], g), 0.0)
    hb_ref[...] = hbuf_ref[...].astype(BF16)
    acc_ref[...] = jnp.zeros(acc_ref.shape, F32)
    lo = FFN_HALO - 1

    def body(c, carry):
        u = _dot(hb_ref[...], wup_ref[c])
        dw = dw_ref[c]
        conv = (dw[0:1] * u[lo:lo + t] + dw[1:2] * u[lo + 1:lo + 1 + t]
                + dw[2:3] * u[lo + 2:lo + 2 + t] + dwb_ref[c])
        act = _gelu_tanh(conv[:, :FFN_CHUNK]) * conv[:, FFN_CHUNK:]
        acc_ref[...] += _dot(act.astype(BF16), wdn_ref[c])
        return carry

    lax.fori_loop(0, wup_ref.shape[0], body, 0)
    o_ref[0] = x_ref[0] + _rms(acc_ref[...], gpost_ref[...])


def _ffn(x, g, wup, dw, dwb, wdn, gpost):
    b, s, d = x.shape
    t = TOK_TILE
    nt = s // t
    hb = t // FFN_HALO
    nc = wup.shape[0]
    return pl.pallas_call(
        _ffn_kernel,
        grid=(b, nt),
        in_specs=[
            pl.BlockSpec((1, t, d), lambda i, j: (i, j, 0)),
            pl.BlockSpec((1, FFN_HALO, d),
                         lambda i, j: (i, jnp.maximum(j * hb - 1, 0), 0)),
            pl.BlockSpec((1, FFN_HALO, d),
                         lambda i, j: (i, jnp.minimum((j + 1) * hb, nt * hb - 1), 0)),
            _resident((1, d)),
            _resident((nc, d, 2 * FFN_CHUNK)),
            _resident((nc, 3, 2 * FFN_CHUNK)),
            _resident((nc, 1, 2 * FFN_CHUNK)),
            _resident((nc, FFN_CHUNK, d)),
            _resident((1, d)),
        ],
        out_specs=pl.BlockSpec((1, t, d), lambda i, j: (i, j, 0)),
        out_shape=jax.ShapeDtypeStruct((b, s, d), F32),
        scratch_shapes=[
            pltpu.VMEM((t + 2 * FFN_HALO, d), F32),
            pltpu.VMEM((t + 2 * FFN_HALO, d), BF16),
            pltpu.VMEM((t, d), F32),
        ],
        compiler_params=pltpu.CompilerParams(
            dimension_semantics=("arbitrary", "arbitrary"),
            vmem_limit_bytes=VMEM_LIMIT),
        name="ffn",
    )(x, x, x, g, wup, dw, dwb, wdn, gpost)


def _rope_tables_t(s):
    rows = s // GRID_W
    r = jnp.repeat(jnp.arange(rows, dtype=F32), GRID_W)
    c = jnp.tile(jnp.arange(GRID_W, dtype=F32), rows)
    axis_dim = HEAD_DIM // 2
    inv = ROPE_THETA ** (-jnp.arange(0, axis_dim, 2, dtype=F32) / axis_dim)
    ang_r = (r[:, None] * inv[None, :]).T
    ang_c = (c[:, None] * inv[None, :]).T
    cos_t = jnp.concatenate([jnp.cos(ang_r)] * 2 + [jnp.cos(ang_c)] * 2, axis=0)
    sr, sc = jnp.sin(ang_r), jnp.sin(ang_c)
    sin_t = jnp.concatenate([-sr, sr, -sc, sc], axis=0)
    return cos_t, sin_t


def _chunk_cols(w, n):
    lead = w.shape[:-1]
    w = w.reshape(lead + (n, FFN_CHUNK))
    return jnp.moveaxis(w, -2, 0)


def kernel(x, mem, norm_mix_pre, w_in, conv_dw, conv_dw_b, conv_ln_g, conv_ln_b,
           q_norm_g, k_norm_g, w_out, norm_mix_post, norm_mem_pre, mem_norm_g,
           w_mem_q, w_mem_kv, w_mem_o, norm_mem_post, norm_ffn_pre, w_up, ffn_dw,
           ffn_dw_b, w_down, norm_ffn_post):
    depth = w_in.shape[0]
    s = x.shape[1]
    assert s % TOK_TILE == 0 and s % ATT_TK == 0 and TOK_TILE % ATT_TK == 0
    assert D_FF % FFN_CHUNK == 0
    nc = D_FF // FFN_CHUNK
    cos_t, sin_t = _rope_tables_t(s)
    row = lambda v: v.reshape(1, -1)
    col = lambda v: jnp.broadcast_to(v[:, None], (v.shape[0], LANES))
    for l in range(depth):
        glu_cols = 2 * CONV_CH
        w_a = w_in[l, :, :glu_cols].astype(BF16)
        w_t = w_in[l, :, glu_cols:].T.astype(BF16)
        a, qt, k, vt = _in_proj(x, row(norm_mix_pre[l]), w_a, w_t, col(q_norm_g[l]),
                                col(k_norm_g[l]), cos_t, sin_t)
        att = _attention(qt, k, vt)
        kt_mem, v_mem = _mem_kv(mem, row(mem_norm_g[l]), w_mem_kv[l].astype(BF16))
        x = _post_attn(x, a, att, conv_dw[l], row(conv_dw_b[l]), row(conv_ln_g[l]),
                       row(conv_ln_b[l]), w_out[l].astype(BF16), row(norm_mix_post[l]),
                       row(norm_mem_pre[l]), w_mem_q[l].astype(BF16), kt_mem, v_mem,
                       w_mem_o[l].astype(BF16), row(norm_mem_post[l]))
        wup = jnp.concatenate([_chunk_cols(w_up[l, :, :D_FF], nc),
                               _chunk_cols(w_up[l, :, D_FF:], nc)], axis=-1).astype(BF16)
        dw = jnp.concatenate([_chunk_cols(ffn_dw[l, :, :D_FF], nc),
                              _chunk_cols(ffn_dw[l, :, D_FF:], nc)], axis=-1)
        dwb = jnp.concatenate([_chunk_cols(ffn_dw_b[l, :D_FF], nc),
                               _chunk_cols(ffn_dw_b[l, D_FF:], nc)], axis=-1)[:, None, :]
        wdn = w_down[l].reshape(nc, FFN_CHUNK, -1).astype(BF16)
        x = _ffn(x, row(norm_ffn_pre[l]), wup, dw, dwb, wdn, row(norm_ffn_post[l]))
    return x
```

```python
import functools
import math

import jax
import jax.numpy as jnp
from jax import lax
from jax.experimental import pallas as pl
from jax.experimental.pallas import tpu as pltpu

F32 = jnp.float32
BF16 = jnp.bfloat16

GRID_W = 64
HEAD_DIM = 64
ATT_HEADS = 8
ATT_KV_HEADS = 2
HEADS_PER_KV = ATT_HEADS // ATT_KV_HEADS
CONV_CH = 512
CONV_K = 31
CONV_PAD = (CONV_K - 1) // 2
MEM_HEADS = 4
MEM_HEAD_DIM = 256
D_FF = 2816
ROPE_THETA = 10000.0
EPS = 1e-6
LOG2E = 1.4426950408889634

LANES = 128
SUBLANES = 8
BF16_ROWS = 16
VMEM_LIMIT = 56 * 1024 * 1024

TOK_TILE = 512
ATT_TQ = 256
ATT_N = HEADS_PER_KV * ATT_TQ
ATT_TK = 256
V_ROWS = HEAD_DIM + BF16_ROWS
CONV_HALO = 16
FFN_HALO = 8
FFN_CHUNK = 256


def _rms(xf, g):
    return xf * lax.rsqrt(jnp.mean(xf * xf, axis=-1, keepdims=True) + EPS) * g


def _dot(a, b):
    return jnp.dot(a, b, preferred_element_type=F32)


def _resident(shape):
    zeros = (0,) * len(shape)
    return pl.BlockSpec(shape, lambda *_: zeros, pipeline_mode=pl.Buffered(1))


def _mem_kv_kernel(mem_ref, g_ref, w_ref, kt_ref, v_ref):
    d = mem_ref.shape[-1]
    mn = _rms(mem_ref[0], g_ref[...]).astype(BF16)
    kv = _dot(mn, w_ref[...])
    kt_ref[0] = (kv[:, :d] * (MEM_HEAD_DIM ** -0.5)).T.astype(BF16)
    v_ref[0] = kv[:, d:].astype(BF16)


def _mem_kv(mem, g, w_kv):
    b, m, d = mem.shape
    return pl.pallas_call(
        _mem_kv_kernel,
        grid=(b,),
        in_specs=[
            pl.BlockSpec((1, m, d), lambda i: (i, 0, 0)),
            _resident((1, d)),
            _resident((d, 2 * d)),
        ],
        out_specs=[
            pl.BlockSpec((1, d, m), lambda i: (i, 0, 0)),
            pl.BlockSpec((1, m, d), lambda i: (i, 0, 0)),
        ],
        out_shape=[
            jax.ShapeDtypeStruct((b, d, m), BF16),
            jax.ShapeDtypeStruct((b, m, d), BF16),
        ],
        compiler_params=pltpu.CompilerParams(
            dimension_semantics=("arbitrary",), vmem_limit_bytes=VMEM_LIMIT),
        name="mem_kv",
    )(mem, g, w_kv)


def _rope_partner(t):
    q = HEAD_DIM // 4
    return jnp.concatenate([t[q:2 * q], t[:q], t[3 * q:], t[2 * q:3 * q]], axis=0)


def _norm_rope_t(zt, g, cos_t, sin_t):
    r = lax.rsqrt(jnp.mean(zt * zt, axis=0, keepdims=True) + EPS)
    y = zt * r * g
    return y * cos_t + _rope_partner(y) * sin_t


def _col_norm2(t_bf16):
    tf = t_bf16.astype(F32)
    return jnp.sum(tf * tf, axis=0, keepdims=True)


def _in_proj_kernel(x_ref, g_ref, wa_ref, wt_ref, gq_ref, gk_ref, cos_ref, sin_ref,
                    a_ref, qt_ref, qn_ref, k_ref, kn2_ref, vt_ref):
    t = x_ref.shape[1]
    h = _rms(x_ref[0], g_ref[...]).astype(BF16)
    za = _dot(h, wa_ref[...])
    a_ref[0] = za[:, :CONV_CH] * jax.nn.sigmoid(za[:, CONV_CH:])
    zt = lax.dot_general(wt_ref[...], h, (((1,), (1,)), ((), ())),
                         preferred_element_type=F32)
    reps = t // LANES
    gq = jnp.concatenate([gq_ref[...]] * reps, axis=1)
    gk = jnp.concatenate([gk_ref[...]] * reps, axis=1)
    cos_t = cos_ref[...]
    sin_t = sin_ref[...]
    qscale = (HEAD_DIM ** -0.5) * LOG2E
    zero = jnp.zeros((HEAD_DIM, ATT_TQ), BF16)
    for hd in range(ATT_HEADS):
        q = _norm_rope_t(zt[hd * HEAD_DIM:(hd + 1) * HEAD_DIM], gq, cos_t, sin_t)
        q = (q * qscale).astype(BF16)
        qn = jnp.sqrt(_col_norm2(q))
        kv, hh = divmod(hd, HEADS_PER_KV)
        cols = slice(hh * ATT_TQ, (hh + 1) * ATT_TQ)
        for c in range(t // ATT_TQ):
            toks = slice(c * ATT_TQ, (c + 1) * ATT_TQ)
            for j in range(ATT_KV_HEADS):
                qt_ref[0, kv, c, j * HEAD_DIM:(j + 1) * HEAD_DIM, cols] = (
                    q[:, toks] if j == kv else zero)
            qn_ref[0, kv, c, :, cols] = qn[:, toks]
    o = ATT_HEADS * HEAD_DIM
    ks = [_norm_rope_t(zt[o + j * HEAD_DIM:o + (j + 1) * HEAD_DIM], gk, cos_t, sin_t)
          for j in range(ATT_KV_HEADS)]
    for j in range(ATT_KV_HEADS):
        kn2_ref[0, j] = _col_norm2(ks[j].astype(BF16))
    k_ref[0] = jnp.concatenate(ks, axis=0).T.astype(BF16)
    o += ATT_KV_HEADS * HEAD_DIM
    row = lax.broadcasted_iota(jnp.int32, (BF16_ROWS, ATT_TK), 0)
    ones_rows = jnp.where(row == 0, 1.0, 0.0).astype(BF16)
    for j in range(ATT_KV_HEADS):
        v = zt[o + j * HEAD_DIM:o + (j + 1) * HEAD_DIM].astype(BF16)
        for c in range(t // ATT_TK):
            vt_ref[0, j, c, :HEAD_DIM, :] = v[:, c * ATT_TK:(c + 1) * ATT_TK]
            vt_ref[0, j, c, HEAD_DIM:, :] = ones_rows


def _in_proj(x, g, w_a, w_t, gq, gk, cos_t, sin_t):
    b, s, d = x.shape
    t = TOK_TILE
    nt = s // t
    qkv_rows = w_t.shape[0]
    return pl.pallas_call(
        _in_proj_kernel,
        grid=(b, nt),
        in_specs=[
            pl.BlockSpec((1, t, d), lambda i, j: (i, j, 0)),
            _resident((1, d)),
            _resident((d, 2 * CONV_CH)),
            _resident((qkv_rows, d)),
            _resident((HEAD_DIM, LANES)),
            _resident((HEAD_DIM, LANES)),
            pl.BlockSpec((HEAD_DIM, t), lambda i, j: (0, j)),
            pl.BlockSpec((HEAD_DIM, t), lambda i, j: (0, j)),
        ],
        out_specs=[
            pl.BlockSpec((1, t, CONV_CH), lambda i, j: (i, j, 0)),
            pl.BlockSpec((1, ATT_KV_HEADS, t // ATT_TQ, 2 * HEAD_DIM, ATT_N),
                         lambda i, j: (i, 0, j, 0, 0)),
            pl.BlockSpec((1, ATT_KV_HEADS, t // ATT_TQ, 1, ATT_N),
                         lambda i, j: (i, 0, j, 0, 0)),
            pl.BlockSpec((1, t, 2 * HEAD_DIM), lambda i, j: (i, j, 0)),
            pl.BlockSpec((1, ATT_KV_HEADS, 1, t), lambda i, j: (i, 0, 0, j)),
            pl.BlockSpec((1, ATT_KV_HEADS, t // ATT_TK, V_ROWS, ATT_TK),
                         lambda i, j: (i, 0, j, 0, 0)),
        ],
        out_shape=[
            jax.ShapeDtypeStruct((b, s, CONV_CH), F32),
            jax.ShapeDtypeStruct((b, ATT_KV_HEADS, s // ATT_TQ, 2 * HEAD_DIM, ATT_N), BF16),
            jax.ShapeDtypeStruct((b, ATT_KV_HEADS, s // ATT_TQ, 1, ATT_N), F32),
            jax.ShapeDtypeStruct((b, s, 2 * HEAD_DIM), BF16),
            jax.ShapeDtypeStruct((b, ATT_KV_HEADS, 1, s), F32),
            jax.ShapeDtypeStruct((b, ATT_KV_HEADS, s // ATT_TK, V_ROWS, ATT_TK), BF16),
        ],
        compiler_params=pltpu.CompilerParams(
            dimension_semantics=("arbitrary", "arbitrary"),
            vmem_limit_bytes=VMEM_LIMIT),
        name="in_proj",
    )(x, g, w_a, w_t, gq, gk, cos_t, sin_t)


ATT_MIN_DENOM = 2.0 ** -80


def _attn_kernel(qt_ref, qn_ref, k_ref, kn2_ref, vt_ref, o_ref, acc_ref, st_ref, m_ref):
    nkb = vt_ref.shape[2]

    def k_block(kb):
        return k_ref[0, pl.ds(pl.multiple_of(kb * ATT_TK, ATT_TK), ATT_TK), :]

    kmax = jnp.sqrt(jnp.max(kn2_ref[0, 0], axis=-1, keepdims=True))
    shift = qn_ref[0, 0, 0] * kmax

    def scores(kb, slot):
        st_ref[slot] = _dot(k_block(kb), qt_ref[0, 0, 0])

    def accumulate(kb, slot):
        p = jnp.exp2(st_ref[slot] - shift).astype(BF16)
        acc_ref[...] += _dot(vt_ref[0, 0, kb], p)

    acc_ref[...] = jnp.zeros(acc_ref.shape, F32)
    scores(0, 0)

    def pair(i, carry):
        kb = 2 * i
        scores(kb + 1, 1)
        accumulate(kb, 0)
        scores(kb + 2, 0)
        accumulate(kb + 1, 1)
        return carry

    lax.fori_loop(0, nkb // 2 - 1, pair, 0)
    scores(nkb - 1, 1)
    accumulate(nkb - 2, 0)
    accumulate(nkb - 1, 1)

    denom_ok = jnp.min(acc_ref[HEAD_DIM:HEAD_DIM + 1, :]) >= ATT_MIN_DENOM

    @pl.when(jnp.logical_not(denom_ok))
    def _():
        m_ref[...] = jnp.full(m_ref.shape, -jnp.inf, F32)
        acc_ref[...] = jnp.zeros(acc_ref.shape, F32)

        def exact(kb, carry):
            st = _dot(k_block(kb), qt_ref[0, 0, 0])
            m_old = m_ref[...]
            m_new = jnp.maximum(m_old, jnp.max(st, axis=0, keepdims=True))
            p = jnp.exp2(st - m_new).astype(BF16)
            acc_ref[...] = acc_ref[...] * jnp.exp2(m_old - m_new) + _dot(vt_ref[0, 0, kb], p)
            m_ref[...] = m_new
            return carry

        lax.fori_loop(0, nkb, exact, 0)

    acc = acc_ref[...]
    outs = [acc[:HEAD_DIM, h * ATT_TQ:(h + 1) * ATT_TQ]
            / acc[HEAD_DIM:HEAD_DIM + 1, h * ATT_TQ:(h + 1) * ATT_TQ]
            for h in range(HEADS_PER_KV)]
    o_ref[0] = jnp.concatenate(outs, axis=0).T.astype(BF16)


def _attention(qt, qn, k, kn2, vt):
    b, s, _ = k.shape
    nkb = vt.shape[2]
    assert nkb % 2 == 0 and nkb >= 4
    width = HEADS_PER_KV * HEAD_DIM
    return pl.pallas_call(
        _attn_kernel,
        grid=(b, ATT_KV_HEADS, s // ATT_TQ),
        in_specs=[
            pl.BlockSpec((1, 1, 1, 2 * HEAD_DIM, ATT_N), lambda i, j, q: (i, j, q, 0, 0)),
            pl.BlockSpec((1, 1, 1, 1, ATT_N), lambda i, j, q: (i, j, q, 0, 0)),
            pl.BlockSpec((1, s, 2 * HEAD_DIM), lambda i, j, q: (i, 0, 0)),
            pl.BlockSpec((1, 1, 1, s), lambda i, j, q: (i, j, 0, 0)),
            pl.BlockSpec((1, 1, nkb, V_ROWS, ATT_TK), lambda i, j, q: (i, j, 0, 0, 0)),
        ],
        out_specs=pl.BlockSpec((1, ATT_TQ, width), lambda i, j, q: (i, q, j)),
        out_shape=jax.ShapeDtypeStruct((b, s, ATT_KV_HEADS * width), BF16),
        scratch_shapes=[
            pltpu.VMEM((V_ROWS, ATT_N), F32),
            pltpu.VMEM((2, ATT_TK, ATT_N), F32),
            pltpu.VMEM((1, ATT_N), F32),
        ],
        compiler_params=pltpu.CompilerParams(
            dimension_semantics=("arbitrary", "arbitrary", "arbitrary"),
            vmem_limit_bytes=VMEM_LIMIT),
        name="attention",
    )(qt, qn, k, kn2, vt)


def _post_attn_kernel(x_ref, a_ref, ap_ref, an_ref, att_ref, dw_ref, dwb_ref,
                      lng_ref, lnb_ref, wout_ref, gpost_ref, gmem_ref, wq_ref,
                      kt_ref, v_ref, wo_ref, gmpost_ref, o_ref, abuf_ref):
    t = x_ref.shape[1]
    j = pl.program_id(1)
    nt = pl.num_programs(1)
    prev = jnp.where(j > 0, ap_ref[0], 0.0)
    nxt = jnp.where(j < nt - 1, an_ref[0], 0.0)
    abuf_ref[:CONV_HALO] = prev
    abuf_ref[CONV_HALO:CONV_HALO + t] = a_ref[0]
    abuf_ref[CONV_HALO + t:] = nxt
    rows = 64
    base = CONV_HALO - CONV_PAD
    for r0 in range(0, t, rows):
        for c0 in range(0, CONV_CH, LANES):
            acc = jnp.broadcast_to(dwb_ref[:, c0:c0 + LANES], (rows, LANES))
            for kk in range(CONV_K):
                acc = acc + (dw_ref[kk:kk + 1, c0:c0 + LANES]
                             * abuf_ref[base + r0 + kk:base + r0 + kk + rows, c0:c0 + LANES])
            abuf_ref[r0:r0 + rows, c0:c0 + LANES] = acc
    c = abuf_ref[:t]
    mu = jnp.mean(c, axis=-1, keepdims=True)
    cc = c - mu
    var = jnp.mean(cc * cc, axis=-1, keepdims=True)
    y = cc * lax.rsqrt(var + EPS) * lng_ref[...] + lnb_ref[...]
    y = y * jax.nn.sigmoid(y)
    mixed = jnp.concatenate([y.astype(BF16), att_ref[0]], axis=-1)
    x1 = x_ref[0] + _rms(_dot(mixed, wout_ref[...]), gpost_ref[...])
    h = _rms(x1, gmem_ref[...]).astype(BF16)
    q = _dot(h, wq_ref[...]).astype(BF16)
    heads = []
    for hd in range(MEM_HEADS):
        sl = slice(hd * MEM_HEAD_DIM, (hd + 1) * MEM_HEAD_DIM)
        s = _dot(q[:, sl], kt_ref[0, sl, :])
        p = jnp.exp(s - jnp.max(s, axis=-1, keepdims=True))
        l = jnp.sum(p, axis=-1, keepdims=True)
        heads.append((_dot(p.astype(BF16), v_ref[0, :, sl]) / l).astype(BF16))
    o = jnp.concatenate(heads, axis=-1)
    o_ref[0] = x1 + _rms(_dot(o, wo_ref[...]), gmpost_ref[...])


def _post_attn(x, a, att, dw, dwb, lng, lnb, wout, gpost, gmem, wq, kt, v, wo, gmpost):
    b, s, d = x.shape
    t = TOK_TILE
    nt = s // t
    hb = t // CONV_HALO
    m = v.shape[1]
    vec = _resident((1, d))
    cvec = _resident((1, CONV_CH))
    return pl.pallas_call(
        _post_attn_kernel,
        grid=(b, nt),
        in_specs=[
            pl.BlockSpec((1, t, d), lambda i, j: (i, j, 0)),
            pl.BlockSpec((1, t, CONV_CH), lambda i, j: (i, j, 0)),
            pl.BlockSpec((1, CONV_HALO, CONV_CH),
                         lambda i, j: (i, jnp.maximum(j * hb - 1, 0), 0)),
            pl.BlockSpec((1, CONV_HALO, CONV_CH),
                         lambda i, j: (i, jnp.minimum((j + 1) * hb, nt * hb - 1), 0)),
            pl.BlockSpec((1, t, CONV_CH), lambda i, j: (i, j, 0)),
            _resident((CONV_K, CONV_CH)),
            cvec, cvec, cvec,
            _resident((d, d)),
            vec, vec,
            _resident((d, d)),
            pl.BlockSpec((1, d, m), lambda i, j: (i, 0, 0)),
            pl.BlockSpec((1, m, d), lambda i, j: (i, 0, 0)),
            _resident((d, d)),
            vec,
        ],
        out_specs=pl.BlockSpec((1, t, d), lambda i, j: (i, j, 0)),
        out_shape=jax.ShapeDtypeStruct((b, s, d), F32),
        scratch_shapes=[pltpu.VMEM((t + 2 * CONV_HALO, CONV_CH), F32)],
        compiler_params=pltpu.CompilerParams(
            dimension_semantics=("arbitrary", "arbitrary"),
            vmem_limit_bytes=VMEM_LIMIT),
        name="post_attn",
    )(x, a, a, a, att, dw, dwb, lng, lnb, wout, gpost, gmem, wq, kt, v, wo, gmpost)


def _gelu_tanh(x):
    c = math.sqrt(2.0 / math.pi)
    return 0.5 * x * (1.0 + jnp.tanh(c * (x + 0.044715 * (x * x * x))))


def _ffn_kernel(x_ref, xp_ref, xn_ref, g_ref, wup_ref, dw_ref, dwb_ref, wdn_ref,
                gpost_ref, o_ref, hbuf_ref, hb_ref, acc_ref):
    t = x_ref.shape[1]
    j = pl.program_id(1)
    nt = pl.num_programs(1)
    g = g_ref[...]
    hbuf_ref[:FFN_HALO] = jnp.where(j > 0, _rms(xp_ref[0], g), 0.0)
    hbuf_ref[FFN_HALO:FFN_HALO + t] = _rms(x_ref[0], g)
    hbuf_ref[FFN_HALO + t:] = jnp.where(j < nt - 1, _rms(xn_ref[0], g), 0.0)
    hb_ref[...] = hbuf_ref[...].astype(BF16)
    acc_ref[...] = jnp.zeros(acc_ref.shape, F32)
    lo = FFN_HALO - 1

    def body(c, carry):
        u = _dot(hb_ref[...], wup_ref[c])
        dw = dw_ref[c]
        conv = (dw[0:1] * u[lo:lo + t] + dw[1:2] * u[lo + 1:lo + 1 + t]
                + dw[2:3] * u[lo + 2:lo + 2 + t] + dwb_ref[c])
        act = _gelu_tanh(conv[:, :FFN_CHUNK]) * conv[:, FFN_CHUNK:]
        acc_ref[...] += _dot(act.astype(BF16), wdn_ref[c])
        return carry

    lax.fori_loop(0, wup_ref.shape[0], body, 0)
    o_ref[0] = x_ref[0] + _rms(acc_ref[...], gpost_ref[...])


def _ffn(x, g, wup, dw, dwb, wdn, gpost):
    b, s, d = x.shape
    t = TOK_TILE
    nt = s // t
    hb = t // FFN_HALO
    nc = wup.shape[0]
    return pl.pallas_call(
        _ffn_kernel,
        grid=(b, nt),
        in_specs=[
            pl.BlockSpec((1, t, d), lambda i, j: (i, j, 0)),
            pl.BlockSpec((1, FFN_HALO, d),
                         lambda i, j: (i, jnp.maximum(j * hb - 1, 0), 0)),
            pl.BlockSpec((1, FFN_HALO, d),
                         lambda i, j: (i, jnp.minimum((j + 1) * hb, nt * hb - 1), 0)),
            _resident((1, d)),
            _resident((nc, d, 2 * FFN_CHUNK)),
            _resident((nc, 3, 2 * FFN_CHUNK)),
            _resident((nc, 1, 2 * FFN_CHUNK)),
            _resident((nc, FFN_CHUNK, d)),
            _resident((1, d)),
        ],
        out_specs=pl.BlockSpec((1, t, d), lambda i, j: (i, j, 0)),
        out_shape=jax.ShapeDtypeStruct((b, s, d), F32),
        scratch_shapes=[
            pltpu.VMEM((t + 2 * FFN_HALO, d), F32),
            pltpu.VMEM((t + 2 * FFN_HALO, d), BF16),
            pltpu.VMEM((t, d), F32),
        ],
        compiler_params=pltpu.CompilerParams(
            dimension_semantics=("arbitrary", "arbitrary"),
            vmem_limit_bytes=VMEM_LIMIT),
        name="ffn",
    )(x, x, x, g, wup, dw, dwb, wdn, gpost)


def _rope_tables_t(s):
    rows = s // GRID_W
    r = jnp.repeat(jnp.arange(rows, dtype=F32), GRID_W)
    c = jnp.tile(jnp.arange(GRID_W, dtype=F32), rows)
    axis_dim = HEAD_DIM // 2
    inv = ROPE_THETA ** (-jnp.arange(0, axis_dim, 2, dtype=F32) / axis_dim)
    ang_r = (r[:, None] * inv[None, :]).T
    ang_c = (c[:, None] * inv[None, :]).T
    cos_t = jnp.concatenate([jnp.cos(ang_r)] * 2 + [jnp.cos(ang_c)] * 2, axis=0)
    sr, sc = jnp.sin(ang_r), jnp.sin(ang_c)
    sin_t = jnp.concatenate([-sr, sr, -sc, sc], axis=0)
    return cos_t, sin_t


def _chunk_cols(w, n):
    lead = w.shape[:-1]
    w = w.reshape(lead + (n, FFN_CHUNK))
    return jnp.moveaxis(w, -2, 0)


def kernel(x, mem, norm_mix_pre, w_in, conv_dw, conv_dw_b, conv_ln_g, conv_ln_b,
           q_norm_g, k_norm_g, w_out, norm_mix_post, norm_mem_pre, mem_norm_g,
           w_mem_q, w_mem_kv, w_mem_o, norm_mem_post, norm_ffn_pre, w_up, ffn_dw,
           ffn_dw_b, w_down, norm_ffn_post):
    depth = w_in.shape[0]
    s = x.shape[1]
    assert s % TOK_TILE == 0 and s % ATT_TK == 0 and TOK_TILE % ATT_TK == 0
    assert D_FF % FFN_CHUNK == 0
    nc = D_FF // FFN_CHUNK
    cos_t, sin_t = _rope_tables_t(s)
    row = lambda v: v.reshape(1, -1)
    col = lambda v: jnp.broadcast_to(v[:, None], (v.shape[0], LANES))
    for l in range(depth):
        glu_cols = 2 * CONV_CH
        w_a = w_in[l, :, :glu_cols].astype(BF16)
        w_t = w_in[l, :, glu_cols:].T.astype(BF16)
        a, qt, qn, k, kn2, vt = _in_proj(x, row(norm_mix_pre[l]), w_a, w_t,
                                         col(q_norm_g[l]), col(k_norm_g[l]), cos_t, sin_t)
        att = _attention(qt, qn, k, kn2, vt)
        kt_mem, v_mem = _mem_kv(mem, row(mem_norm_g[l]), w_mem_kv[l].astype(BF16))
        x = _post_attn(x, a, att, conv_dw[l], row(conv_dw_b[l]), row(conv_ln_g[l]),
                       row(conv_ln_b[l]), w_out[l].astype(BF16), row(norm_mix_post[l]),
                       row(norm_mem_pre[l]), w_mem_q[l].astype(BF16), kt_mem, v_mem,
                       w_mem_o[l].astype(BF16), row(norm_mem_post[l]))
        wup = jnp.concatenate([_chunk_cols(w_up[l, :, :D_FF], nc),
                               _chunk_cols(w_up[l, :, D_FF:], nc)], axis=-1).astype(BF16)
        dw = jnp.concatenate([_chunk_cols(ffn_dw[l, :, :D_FF], nc),
                              _chunk_cols(ffn_dw[l, :, D_FF:], nc)], axis=-1)
        dwb = jnp.concatenate([_chunk_cols(ffn_dw_b[l, :D_FF], nc),
                               _chunk_cols(ffn_dw_b[l, D_FF:], nc)], axis=-1)[:, None, :]
        wdn = w_down[l].reshape(nc, FFN_CHUNK, -1).astype(BF16)
        x = _ffn(x, row(norm_ffn_pre[l]), wup, dw, dwb, wdn, row(norm_ffn_post[l]))
    return x
```

```python
import functools
import math

import jax
import jax.numpy as jnp
from jax import lax
from jax.experimental import pallas as pl
from jax.experimental.pallas import tpu as pltpu

F32 = jnp.float32
BF16 = jnp.bfloat16

GRID_W = 64
HEAD_DIM = 64
ATT_HEADS = 8
ATT_KV_HEADS = 2
HEADS_PER_KV = ATT_HEADS // ATT_KV_HEADS
CONV_CH = 512
CONV_K = 31
CONV_PAD = (CONV_K - 1) // 2
MEM_HEADS = 4
MEM_HEAD_DIM = 256
D_FF = 2816
ROPE_THETA = 10000.0
EPS = 1e-6
LOG2E = 1.4426950408889634

LANES = 128
SUBLANES = 8
BF16_ROWS = 16
VMEM_LIMIT = 56 * 1024 * 1024

TOK_TILE = 512
FFN_TILE = 256
ATT_TQ = 256
ATT_N = HEADS_PER_KV * ATT_TQ
ATT_TK = 256
ATT_UNROLL = 8
V_ROWS = HEAD_DIM + BF16_ROWS
CONV_HALO = 16
FFN_HALO = 8
FFN_CHUNK = 256


def _rms(xf, g):
    return xf * lax.rsqrt(jnp.mean(xf * xf, axis=-1, keepdims=True) + EPS) * g


def _dot(a, b):
    return jnp.dot(a, b, preferred_element_type=F32)


def _resident(shape):
    zeros = (0,) * len(shape)
    return pl.BlockSpec(shape, lambda *_: zeros, pipeline_mode=pl.Buffered(1))


def _mem_kv_kernel(mem_ref, g_ref, w_ref, kt_ref, v_ref):
    d = mem_ref.shape[-1]
    mn = _rms(mem_ref[0], g_ref[...]).astype(BF16)
    kv = _dot(mn, w_ref[...])
    kt_ref[0] = (kv[:, :d] * (MEM_HEAD_DIM ** -0.5)).T.astype(BF16)
    v_ref[0] = kv[:, d:].astype(BF16)


def _mem_kv(mem, g, w_kv):
    b, m, d = mem.shape
    return pl.pallas_call(
        _mem_kv_kernel,
        grid=(b,),
        in_specs=[
            pl.BlockSpec((1, m, d), lambda i: (i, 0, 0)),
            _resident((1, d)),
            _resident((d, 2 * d)),
        ],
        out_specs=[
            pl.BlockSpec((1, d, m), lambda i: (i, 0, 0)),
            pl.BlockSpec((1, m, d), lambda i: (i, 0, 0)),
        ],
        out_shape=[
            jax.ShapeDtypeStruct((b, d, m), BF16),
            jax.ShapeDtypeStruct((b, m, d), BF16),
        ],
        compiler_params=pltpu.CompilerParams(
            dimension_semantics=("arbitrary",), vmem_limit_bytes=VMEM_LIMIT),
        name="mem_kv",
    )(mem, g, w_kv)


def _rope_partner(t):
    q = HEAD_DIM // 4
    return jnp.concatenate([t[q:2 * q], t[:q], t[3 * q:], t[2 * q:3 * q]], axis=0)


def _norm_rope_t(zt, g, cos_t, sin_t):
    r = lax.rsqrt(jnp.mean(zt * zt, axis=0, keepdims=True) + EPS)
    y = zt * r * g
    return y * cos_t + _rope_partner(y) * sin_t


def _col_norm2(t_bf16):
    tf = t_bf16.astype(F32)
    return jnp.sum(tf * tf, axis=0, keepdims=True)


def _in_proj_kernel(x_ref, g_ref, wa_ref, wt_ref, gq_ref, gk_ref, cos_ref, sin_ref,
                    a_ref, qt_ref, qn_ref, k_ref, kn2_ref, vt_ref):
    t = x_ref.shape[1]
    h = _rms(x_ref[0], g_ref[...]).astype(BF16)
    za = _dot(h, wa_ref[...])
    a_ref[0] = za[:, :CONV_CH] * jax.nn.sigmoid(za[:, CONV_CH:])
    zt = lax.dot_general(wt_ref[...], h, (((1,), (1,)), ((), ())),
                         preferred_element_type=F32)
    reps = t // LANES
    gq = jnp.concatenate([gq_ref[...]] * reps, axis=1)
    gk = jnp.concatenate([gk_ref[...]] * reps, axis=1)
    cos_t = cos_ref[...]
    sin_t = sin_ref[...]
    qscale = (HEAD_DIM ** -0.5) * LOG2E
    zero = jnp.zeros((HEAD_DIM, ATT_TQ), BF16)
    for hd in range(ATT_HEADS):
        q = _norm_rope_t(zt[hd * HEAD_DIM:(hd + 1) * HEAD_DIM], gq, cos_t, sin_t)
        q = (q * qscale).astype(BF16)
        qn = jnp.sqrt(_col_norm2(q))
        kv, hh = divmod(hd, HEADS_PER_KV)
        cols = slice(hh * ATT_TQ, (hh + 1) * ATT_TQ)
        for c in range(t // ATT_TQ):
            toks = slice(c * ATT_TQ, (c + 1) * ATT_TQ)
            for j in range(ATT_KV_HEADS):
                qt_ref[0, kv, c, j * HEAD_DIM:(j + 1) * HEAD_DIM, cols] = (
                    q[:, toks] if j == kv else zero)
            qn_ref[0, kv, c, :, cols] = qn[:, toks]
    o = ATT_HEADS * HEAD_DIM
    ks = [_norm_rope_t(zt[o + j * HEAD_DIM:o + (j + 1) * HEAD_DIM], gk, cos_t, sin_t)
          for j in range(ATT_KV_HEADS)]
    for j in range(ATT_KV_HEADS):
        kn2_ref[0, j] = _col_norm2(ks[j].astype(BF16))
    k_ref[0] = jnp.concatenate(ks, axis=0).T.astype(BF16)
    o += ATT_KV_HEADS * HEAD_DIM
    row = lax.broadcasted_iota(jnp.int32, (BF16_ROWS, ATT_TK), 0)
    ones_rows = jnp.where(row == 0, 1.0, 0.0).astype(BF16)
    for j in range(ATT_KV_HEADS):
        v = zt[o + j * HEAD_DIM:o + (j + 1) * HEAD_DIM].astype(BF16)
        for c in range(t // ATT_TK):
            vt_ref[0, j, c, :HEAD_DIM, :] = v[:, c * ATT_TK:(c + 1) * ATT_TK]
            vt_ref[0, j, c, HEAD_DIM:, :] = ones_rows


def _in_proj(x, g, w_a, w_t, gq, gk, cos_t, sin_t):
    b, s, d = x.shape
    t = TOK_TILE
    nt = s // t
    qkv_rows = w_t.shape[0]
    return pl.pallas_call(
        _in_proj_kernel,
        grid=(b, nt),
        in_specs=[
            pl.BlockSpec((1, t, d), lambda i, j: (i, j, 0)),
            _resident((1, d)),
            _resident((d, 2 * CONV_CH)),
            _resident((qkv_rows, d)),
            _resident((HEAD_DIM, LANES)),
            _resident((HEAD_DIM, LANES)),
            pl.BlockSpec((HEAD_DIM, t), lambda i, j: (0, j)),
            pl.BlockSpec((HEAD_DIM, t), lambda i, j: (0, j)),
        ],
        out_specs=[
            pl.BlockSpec((1, t, CONV_CH), lambda i, j: (i, j, 0)),
            pl.BlockSpec((1, ATT_KV_HEADS, t // ATT_TQ, 2 * HEAD_DIM, ATT_N),
                         lambda i, j: (i, 0, j, 0, 0)),
            pl.BlockSpec((1, ATT_KV_HEADS, t // ATT_TQ, 1, ATT_N),
                         lambda i, j: (i, 0, j, 0, 0)),
            pl.BlockSpec((1, t, 2 * HEAD_DIM), lambda i, j: (i, j, 0)),
            pl.BlockSpec((1, ATT_KV_HEADS, 1, t), lambda i, j: (i, 0, 0, j)),
            pl.BlockSpec((1, ATT_KV_HEADS, t // ATT_TK, V_ROWS, ATT_TK),
                         lambda i, j: (i, 0, j, 0, 0)),
        ],
        out_shape=[
            jax.ShapeDtypeStruct((b, s, CONV_CH), F32),
            jax.ShapeDtypeStruct((b, ATT_KV_HEADS, s // ATT_TQ, 2 * HEAD_DIM, ATT_N), BF16),
            jax.ShapeDtypeStruct((b, ATT_KV_HEADS, s // ATT_TQ, 1, ATT_N), F32),
            jax.ShapeDtypeStruct((b, s, 2 * HEAD_DIM), BF16),
            jax.ShapeDtypeStruct((b, ATT_KV_HEADS, 1, s), F32),
            jax.ShapeDtypeStruct((b, ATT_KV_HEADS, s // ATT_TK, V_ROWS, ATT_TK), BF16),
        ],
        compiler_params=pltpu.CompilerParams(
            dimension_semantics=("arbitrary", "arbitrary"),
            vmem_limit_bytes=VMEM_LIMIT),
        name="in_proj",
    )(x, g, w_a, w_t, gq, gk, cos_t, sin_t)


ATT_MIN_DENOM = 2.0 ** -80


def _attn_kernel(qt_ref, qn_ref, k_ref, kn2_ref, vt_ref, o_ref, acc_ref, st_ref, m_ref):
    nkb = vt_ref.shape[2]

    def k_block(kb):
        return k_ref[0, pl.ds(pl.multiple_of(kb * ATT_TK, ATT_TK), ATT_TK), :]

    kmax = jnp.sqrt(jnp.max(kn2_ref[0, 0], axis=-1, keepdims=True))
    shift = qn_ref[0, 0, 0] * kmax

    def scores(kb, slot):
        st_ref[slot] = _dot(k_block(kb), qt_ref[0, 0, 0])

    def accumulate(kb, slot):
        p = jnp.exp2(st_ref[slot] - shift).astype(BF16)
        acc_ref[...] += _dot(vt_ref[0, 0, kb], p)

    acc_ref[...] = jnp.zeros(acc_ref.shape, F32)
    scores(0, 0)

    def group(i, carry):
        for u in range(ATT_UNROLL):
            kb = ATT_UNROLL * i + u
            scores(kb + 1, (u + 1) % 2)
            accumulate(kb, u % 2)
        return carry

    lax.fori_loop(0, nkb // ATT_UNROLL - 1, group, 0)
    for kb in range(nkb - ATT_UNROLL, nkb):
        if kb + 1 < nkb:
            scores(kb + 1, (kb + 1) % 2)
        accumulate(kb, kb % 2)

    denom_ok = jnp.min(acc_ref[HEAD_DIM:HEAD_DIM + 1, :]) >= ATT_MIN_DENOM

    @pl.when(jnp.logical_not(denom_ok))
    def _():
        m_ref[...] = jnp.full(m_ref.shape, -jnp.inf, F32)
        acc_ref[...] = jnp.zeros(acc_ref.shape, F32)

        def exact(kb, carry):
            st = _dot(k_block(kb), qt_ref[0, 0, 0])
            m_old = m_ref[...]
            m_new = jnp.maximum(m_old, jnp.max(st, axis=0, keepdims=True))
            p = jnp.exp2(st - m_new).astype(BF16)
            acc_ref[...] = acc_ref[...] * jnp.exp2(m_old - m_new) + _dot(vt_ref[0, 0, kb], p)
            m_ref[...] = m_new
            return carry

        lax.fori_loop(0, nkb, exact, 0)

    acc = acc_ref[...]
    outs = [acc[:HEAD_DIM, h * ATT_TQ:(h + 1) * ATT_TQ]
            / acc[HEAD_DIM:HEAD_DIM + 1, h * ATT_TQ:(h + 1) * ATT_TQ]
            for h in range(HEADS_PER_KV)]
    o_ref[0] = jnp.concatenate(outs, axis=0).T.astype(BF16)


def _attention(qt, qn, k, kn2, vt):
    b, s, _ = k.shape
    nkb = vt.shape[2]
    assert nkb % ATT_UNROLL == 0 and nkb >= 2 * ATT_UNROLL
    width = HEADS_PER_KV * HEAD_DIM
    return pl.pallas_call(
        _attn_kernel,
        grid=(b, ATT_KV_HEADS, s // ATT_TQ),
        in_specs=[
            pl.BlockSpec((1, 1, 1, 2 * HEAD_DIM, ATT_N), lambda i, j, q: (i, j, q, 0, 0)),
            pl.BlockSpec((1, 1, 1, 1, ATT_N), lambda i, j, q: (i, j, q, 0, 0)),
            pl.BlockSpec((1, s, 2 * HEAD_DIM), lambda i, j, q: (i, 0, 0)),
            pl.BlockSpec((1, 1, 1, s), lambda i, j, q: (i, j, 0, 0)),
            pl.BlockSpec((1, 1, nkb, V_ROWS, ATT_TK), lambda i, j, q: (i, j, 0, 0, 0)),
        ],
        out_specs=pl.BlockSpec((1, ATT_TQ, width), lambda i, j, q: (i, q, j)),
        out_shape=jax.ShapeDtypeStruct((b, s, ATT_KV_HEADS * width), BF16),
        scratch_shapes=[
            pltpu.VMEM((V_ROWS, ATT_N), F32),
            pltpu.VMEM((2, ATT_TK, ATT_N), F32),
            pltpu.VMEM((1, ATT_N), F32),
        ],
        compiler_params=pltpu.CompilerParams(
            dimension_semantics=("arbitrary", "arbitrary", "arbitrary"),
            vmem_limit_bytes=VMEM_LIMIT),
        name="attention",
    )(qt, qn, k, kn2, vt)


def _post_attn_kernel(x_ref, a_ref, ap_ref, an_ref, att_ref, dw_ref, dwb_ref,
                      lng_ref, lnb_ref, wout_ref, gpost_ref, gmem_ref, wq_ref,
                      kt_ref, v_ref, wo_ref, gmpost_ref, o_ref, abuf_ref, ash_ref, cbuf_ref):
    t = x_ref.shape[1]
    j = pl.program_id(1)
    nt = pl.num_programs(1)
    prev = jnp.where(j > 0, ap_ref[0], 0.0)
    nxt = jnp.where(j < nt - 1, an_ref[0], 0.0)
    abuf_ref[:CONV_HALO] = prev
    abuf_ref[CONV_HALO:CONV_HALO + t] = a_ref[0]
    abuf_ref[CONV_HALO + t:] = nxt
    nrows = t + 2 * CONV_HALO
    padded = abuf_ref[...]
    for r in range(1, SUBLANES):
        ash_ref[r - 1] = pltpu.roll(padded, nrows - r, axis=0)
    rows = 64
    base = CONV_HALO - CONV_PAD
    for r0 in range(0, t, rows):
        for c0 in range(0, CONV_CH, LANES):
            lanes = slice(c0, c0 + LANES)
            acc = jnp.broadcast_to(dwb_ref[:, lanes], (rows, LANES))
            for kk in range(CONV_K):
                al, r = divmod(base + kk, SUBLANES)
                lo = al * SUBLANES + r0
                src = abuf_ref if r == 0 else ash_ref.at[r - 1]
                acc = acc + dw_ref[kk:kk + 1, lanes] * src[lo:lo + rows, lanes]
            cbuf_ref[r0:r0 + rows, lanes] = acc
    c = cbuf_ref[...]
    mu = jnp.mean(c, axis=-1, keepdims=True)
    cc = c - mu
    var = jnp.mean(cc * cc, axis=-1, keepdims=True)
    y = cc * lax.rsqrt(var + EPS) * lng_ref[...] + lnb_ref[...]
    y = y * jax.nn.sigmoid(y)
    mixed = jnp.concatenate([y.astype(BF16), att_ref[0]], axis=-1)
    x1 = x_ref[0] + _rms(_dot(mixed, wout_ref[...]), gpost_ref[...])
    h = _rms(x1, gmem_ref[...]).astype(BF16)
    q = _dot(h, wq_ref[...]).astype(BF16)
    heads = []
    for hd in range(MEM_HEADS):
        sl = slice(hd * MEM_HEAD_DIM, (hd + 1) * MEM_HEAD_DIM)
        s = _dot(q[:, sl], kt_ref[0, sl, :])
        p = jnp.exp(s - jnp.max(s, axis=-1, keepdims=True))
        l = jnp.sum(p, axis=-1, keepdims=True)
        heads.append((_dot(p.astype(BF16), v_ref[0, :, sl]) / l).astype(BF16))
    o = jnp.concatenate(heads, axis=-1)
    o_ref[0] = x1 + _rms(_dot(o, wo_ref[...]), gmpost_ref[...])


def _post_attn(x, a, att, dw, dwb, lng, lnb, wout, gpost, gmem, wq, kt, v, wo, gmpost):
    b, s, d = x.shape
    t = TOK_TILE
    nt = s // t
    hb = t // CONV_HALO
    m = v.shape[1]
    vec = _resident((1, d))
    cvec = _resident((1, CONV_CH))
    return pl.pallas_call(
        _post_attn_kernel,
        grid=(b, nt),
        in_specs=[
            pl.BlockSpec((1, t, d), lambda i, j: (i, j, 0)),
            pl.BlockSpec((1, t, CONV_CH), lambda i, j: (i, j, 0)),
            pl.BlockSpec((1, CONV_HALO, CONV_CH),
                         lambda i, j: (i, jnp.maximum(j * hb - 1, 0), 0)),
            pl.BlockSpec((1, CONV_HALO, CONV_CH),
                         lambda i, j: (i, jnp.minimum((j + 1) * hb, nt * hb - 1), 0)),
            pl.BlockSpec((1, t, CONV_CH), lambda i, j: (i, j, 0)),
            _resident((CONV_K, CONV_CH)),
            cvec, cvec, cvec,
            _resident((d, d)),
            vec, vec,
            _resident((d, d)),
            pl.BlockSpec((1, d, m), lambda i, j: (i, 0, 0)),
            pl.BlockSpec((1, m, d), lambda i, j: (i, 0, 0)),
            _resident((d, d)),
            vec,
        ],
        out_specs=pl.BlockSpec((1, t, d), lambda i, j: (i, j, 0)),
        out_shape=jax.ShapeDtypeStruct((b, s, d), F32),
        scratch_shapes=[
            pltpu.VMEM((t + 2 * CONV_HALO, CONV_CH), F32),
            pltpu.VMEM((SUBLANES - 1, t + 2 * CONV_HALO, CONV_CH), F32),
            pltpu.VMEM((t, CONV_CH), F32),
        ],
        compiler_params=pltpu.CompilerParams(
            dimension_semantics=("arbitrary", "arbitrary"),
            vmem_limit_bytes=VMEM_LIMIT),
        name="post_attn",
    )(x, a, a, a, att, dw, dwb, lng, lnb, wout, gpost, gmem, wq, kt, v, wo, gmpost)


def _gelu_tanh(x):
    c = math.sqrt(2.0 / math.pi)
    return 0.5 * x * (1.0 + jnp.tanh(c * (x + 0.044715 * (x * x * x))))


def _ffn_kernel(x_ref, xp_ref, xn_ref, g_ref, wup_ref, dw_ref, dwb_ref, wdn_ref,
                gpost_ref, o_ref, hbuf_ref, hb_ref, acc_ref, u_ref):
    t = x_ref.shape[1]
    j = pl.program_id(1)
    nt = pl.num_programs(1)
    g = g_ref[...]
    hbuf_ref[:FFN_HALO] = jnp.where(j > 0, _rms(xp_ref[0], g), 0.0)
    hbuf_ref[FFN_HALO:FFN_HALO + t] = _rms(x_ref[0], g)
    hbuf_ref[FFN_HALO + t:] = jnp.where(j < nt - 1, _rms(xn_ref[0], g), 0.0)
    hb_ref[...] = hbuf_ref[...].astype(BF16)
    acc_ref[...] = jnp.zeros(acc_ref.shape, F32)
    rows = t + 2 * FFN_HALO
    nc = wup_ref.shape[0]

    def up(c, slot):
        u_ref[slot] = _dot(hb_ref[...], wup_ref[c])

    def down(c, slot):
        u = u_ref[slot]
        dw = dw_ref[c]
        prev = pltpu.roll(u, 1, axis=0)
        nxt = pltpu.roll(u, rows - 1, axis=0)
        mid = slice(FFN_HALO, FFN_HALO + t)
        conv = dw[0:1] * prev[mid] + dw[1:2] * u[mid] + dw[2:3] * nxt[mid] + dwb_ref[c]
        act = _gelu_tanh(conv[:, :FFN_CHUNK]) * conv[:, FFN_CHUNK:]
        acc_ref[...] += _dot(act.astype(BF16), wdn_ref[c])

    up(0, 0)
    for c in range(nc):
        if c + 1 < nc:
            up(c + 1, (c + 1) % 2)
        down(c, c % 2)
    o_ref[0] = x_ref[0] + _rms(acc_ref[...], gpost_ref[...])


def _ffn(x, g, wup, dw, dwb, wdn, gpost):
    b, s, d = x.shape
    t = FFN_TILE
    nt = s // t
    hb = t // FFN_HALO
    nc = wup.shape[0]
    return pl.pallas_call(
        _ffn_kernel,
        grid=(b, nt),
        in_specs=[
            pl.BlockSpec((1, t, d), lambda i, j: (i, j, 0)),
            pl.BlockSpec((1, FFN_HALO, d),
                         lambda i, j: (i, jnp.maximum(j * hb - 1, 0), 0)),
            pl.BlockSpec((1, FFN_HALO, d),
                         lambda i, j: (i, jnp.minimum((j + 1) * hb, nt * hb - 1), 0)),
            _resident((1, d)),
            _resident((nc, d, 2 * FFN_CHUNK)),
            _resident((nc, 3, 2 * FFN_CHUNK)),
            _resident((nc, 1, 2 * FFN_CHUNK)),
            _resident((nc, FFN_CHUNK, d)),
            _resident((1, d)),
        ],
        out_specs=pl.BlockSpec((1, t, d), lambda i, j: (i, j, 0)),
        out_shape=jax.ShapeDtypeStruct((b, s, d), F32),
        scratch_shapes=[
            pltpu.VMEM((t + 2 * FFN_HALO, d), F32),
            pltpu.VMEM((t + 2 * FFN_HALO, d), BF16),
            pltpu.VMEM((t, d), F32),
            pltpu.VMEM((2, t + 2 * FFN_HALO, 2 * FFN_CHUNK), F32),
        ],
        compiler_params=pltpu.CompilerParams(
            dimension_semantics=("arbitrary", "arbitrary"),
            vmem_limit_bytes=VMEM_LIMIT),
        name="ffn",
    )(x, x, x, g, wup, dw, dwb, wdn, gpost)


def _rope_tables_t(s):
    rows = s // GRID_W
    r = jnp.repeat(jnp.arange(rows, dtype=F32), GRID_W)
    c = jnp.tile(jnp.arange(GRID_W, dtype=F32), rows)
    axis_dim = HEAD_DIM // 2
    inv = ROPE_THETA ** (-jnp.arange(0, axis_dim, 2, dtype=F32) / axis_dim)
    ang_r = (r[:, None] * inv[None, :]).T
    ang_c = (c[:, None] * inv[None, :]).T
    cos_t = jnp.concatenate([jnp.cos(ang_r)] * 2 + [jnp.cos(ang_c)] * 2, axis=0)
    sr, sc = jnp.sin(ang_r), jnp.sin(ang_c)
    sin_t = jnp.concatenate([-sr, sr, -sc, sc], axis=0)
    return cos_t, sin_t


def _chunk_cols(w, n):
    lead = w.shape[:-1]
    w = w.reshape(lead + (n, FFN_CHUNK))
    return jnp.moveaxis(w, -2, 0)


def kernel(x, mem, norm_mix_pre, w_in, conv_dw, conv_dw_b, conv_ln_g, conv_ln_b,
           q_norm_g, k_norm_g, w_out, norm_mix_post, norm_mem_pre, mem_norm_g,
           w_mem_q, w_mem_kv, w_mem_o, norm_mem_post, norm_ffn_pre, w_up, ffn_dw,
           ffn_dw_b, w_down, norm_ffn_post):
    depth = w_in.shape[0]
    s = x.shape[1]
    assert s % TOK_TILE == 0 and s % ATT_TK == 0 and TOK_TILE % ATT_TK == 0
    assert D_FF % FFN_CHUNK == 0
    nc = D_FF // FFN_CHUNK
    cos_t, sin_t = _rope_tables_t(s)
    row = lambda v: v.reshape(1, -1)
    col = lambda v: jnp.broadcast_to(v[:, None], (v.shape[0], LANES))
    for l in range(depth):
        glu_cols = 2 * CONV_CH
        w_a = w_in[l, :, :glu_cols].astype(BF16)
        w_t = w_in[l, :, glu_cols:].T.astype(BF16)
        a, qt, qn, k, kn2, vt = _in_proj(x, row(norm_mix_pre[l]), w_a, w_t,
                                         col(q_norm_g[l]), col(k_norm_g[l]), cos_t, sin_t)
        att = _attention(qt, qn, k, kn2, vt)
        kt_mem, v_mem = _mem_kv(mem, row(mem_norm_g[l]), w_mem_kv[l].astype(BF16))
        x = _post_attn(x, a, att, conv_dw[l], row(conv_dw_b[l]), row(conv_ln_g[l]),
                       row(conv_ln_b[l]), w_out[l].astype(BF16), row(norm_mix_post[l]),
                       row(norm_mem_pre[l]), w_mem_q[l].astype(BF16), kt_mem, v_mem,
                       w_mem_o[l].astype(BF16), row(norm_mem_post[l]))
        wup = jnp.concatenate([_chunk_cols(w_up[l, :, :D_FF], nc),
                               _chunk_cols(w_up[l, :, D_FF:], nc)], axis=-1).astype(BF16)
        dw = jnp.concatenate([_chunk_cols(ffn_dw[l, :, :D_FF], nc),
                              _chunk_cols(ffn_dw[l, :, D_FF:], nc)], axis=-1)
        dwb = jnp.concatenate([_chunk_cols(ffn_dw_b[l, :D_FF], nc),
                               _chunk_cols(ffn_dw_b[l, D_FF:], nc)], axis=-1)[:, None, :]
        wdn = w_down[l].reshape(nc, FFN_CHUNK, -1).astype(BF16)
        x = _ffn(x, row(norm_ffn_pre[l]), wup, dw, dwb, wdn, row(norm_ffn_post[l]))
    return x
```

```python
import math

import jax
import jax.numpy as jnp
import numpy as np
from jax import lax
from jax.experimental import pallas as pl
from jax.experimental.pallas import tpu as pltpu

F32 = jnp.float32
BF16 = jnp.bfloat16

GRID_W = 64
HEAD_DIM = 64
ATT_HEADS = 8
ATT_KV_HEADS = 2
HEADS_PER_KV = ATT_HEADS // ATT_KV_HEADS
CONV_CH = 512
CONV_K = 31
CONV_PAD = (CONV_K - 1) // 2
MEM_HEADS = 4
MEM_HEAD_DIM = 256
ROPE_THETA = 10000.0
EPS = 1e-6
LOG2E = 1.4426950408889634

LANES = 128
SUBLANES = 8
BF16_ROWS = 16
VMEM_LIMIT = 56 * 1024 * 1024

TOK_TILE = 512
FFN_TILE = 256
ATT_TQ = 256
ATT_N = HEADS_PER_KV * ATT_TQ
ATT_TK = 256
ATT_UNROLL = 8
V_ROWS = HEAD_DIM + BF16_ROWS
CONV_HALO = 16
FFN_HALO = 8
FFN_CHUNK = 256


def _rms(xf, g):
    return xf * lax.rsqrt(jnp.mean(xf * xf, axis=-1, keepdims=True) + EPS) * g


def _dot(a, b):
    return jnp.dot(a, b, preferred_element_type=F32)


def _resident(shape):
    zeros = (0,) * len(shape)
    return pl.BlockSpec(shape, lambda *_: zeros, pipeline_mode=pl.Buffered(1))


def _mem_kv_kernel(mem_ref, g_ref, w_ref, kt_ref, v_ref):
    d = mem_ref.shape[-1]
    mn = _rms(mem_ref[0], g_ref[...]).astype(BF16)
    kv = _dot(mn, w_ref[...])
    kt_ref[0] = (kv[:, :d] * (MEM_HEAD_DIM ** -0.5)).T.astype(BF16)
    v_ref[0] = kv[:, d:].astype(BF16)


def _mem_kv(mem, g, w_kv):
    b, m, d = mem.shape
    return pl.pallas_call(
        _mem_kv_kernel,
        grid=(b,),
        in_specs=[
            pl.BlockSpec((1, m, d), lambda i: (i, 0, 0)),
            _resident((1, d)),
            _resident((d, 2 * d)),
        ],
        out_specs=[
            pl.BlockSpec((1, d, m), lambda i: (i, 0, 0)),
            pl.BlockSpec((1, m, d), lambda i: (i, 0, 0)),
        ],
        out_shape=[
            jax.ShapeDtypeStruct((b, d, m), BF16),
            jax.ShapeDtypeStruct((b, m, d), BF16),
        ],
        compiler_params=pltpu.CompilerParams(
            dimension_semantics=("arbitrary",), vmem_limit_bytes=VMEM_LIMIT),
        name="mem_kv",
    )(mem, g, w_kv)


def _rope_partner(t):
    q = HEAD_DIM // 4
    return jnp.concatenate([t[q:2 * q], t[:q], t[3 * q:], t[2 * q:3 * q]], axis=0)


def _norm_rope_t(zt, g, cos_t, sin_t):
    r = lax.rsqrt(jnp.mean(zt * zt, axis=0, keepdims=True) + EPS)
    y = zt * r * g
    return y * cos_t + _rope_partner(y) * sin_t


def _col_norm2(t_bf16):
    tf = t_bf16.astype(F32)
    return jnp.sum(tf * tf, axis=0, keepdims=True)


def _in_proj_kernel(x_ref, g_ref, wa_ref, wt_ref, gq_ref, gk_ref, cos_ref, sin_ref,
                    a_ref, qt_ref, qn_ref, k_ref, kn2_ref, vt_ref):
    t = x_ref.shape[1]
    h = _rms(x_ref[0], g_ref[...]).astype(BF16)
    za = _dot(h, wa_ref[...])
    a_ref[0] = za[:, :CONV_CH] * jax.nn.sigmoid(za[:, CONV_CH:])
    zt = lax.dot_general(wt_ref[...], h, (((1,), (1,)), ((), ())),
                         preferred_element_type=F32)
    reps = t // LANES
    gq = jnp.concatenate([gq_ref[...]] * reps, axis=1)
    gk = jnp.concatenate([gk_ref[...]] * reps, axis=1)
    cos_t = cos_ref[...]
    sin_t = sin_ref[...]
    qscale = (HEAD_DIM ** -0.5) * LOG2E
    zero = jnp.zeros((HEAD_DIM, ATT_TQ), BF16)
    for hd in range(ATT_HEADS):
        q = _norm_rope_t(zt[hd * HEAD_DIM:(hd + 1) * HEAD_DIM], gq, cos_t, sin_t)
        q = (q * qscale).astype(BF16)
        qn = jnp.sqrt(_col_norm2(q))
        kv, hh = divmod(hd, HEADS_PER_KV)
        cols = slice(hh * ATT_TQ, (hh + 1) * ATT_TQ)
        for c in range(t // ATT_TQ):
            toks = slice(c * ATT_TQ, (c + 1) * ATT_TQ)
            for j in range(ATT_KV_HEADS):
                qt_ref[0, kv, c, j * HEAD_DIM:(j + 1) * HEAD_DIM, cols] = (
                    q[:, toks] if j == kv else zero)
            qn_ref[0, kv, c, :, cols] = qn[:, toks]
    o = ATT_HEADS * HEAD_DIM
    ks = [_norm_rope_t(zt[o + j * HEAD_DIM:o + (j + 1) * HEAD_DIM], gk, cos_t, sin_t)
          for j in range(ATT_KV_HEADS)]
    for j in range(ATT_KV_HEADS):
        kn2_ref[0, j] = _col_norm2(ks[j].astype(BF16))
    k_ref[0] = jnp.concatenate(ks, axis=0).T.astype(BF16)
    o += ATT_KV_HEADS * HEAD_DIM
    row = lax.broadcasted_iota(jnp.int32, (BF16_ROWS, ATT_TK), 0)
    ones_rows = jnp.where(row == 0, 1.0, 0.0).astype(BF16)
    for j in range(ATT_KV_HEADS):
        v = zt[o + j * HEAD_DIM:o + (j + 1) * HEAD_DIM].astype(BF16)
        for c in range(t // ATT_TK):
            vt_ref[0, j, c, :HEAD_DIM, :] = v[:, c * ATT_TK:(c + 1) * ATT_TK]
            vt_ref[0, j, c, HEAD_DIM:, :] = ones_rows


def _in_proj(x, g, w_a, w_t, gq, gk, cos_t, sin_t):
    b, s, d = x.shape
    t = TOK_TILE
    nt = s // t
    qkv_rows = w_t.shape[0]
    return pl.pallas_call(
        _in_proj_kernel,
        grid=(b, nt),
        in_specs=[
            pl.BlockSpec((1, t, d), lambda i, j: (i, j, 0)),
            _resident((1, d)),
            _resident((d, 2 * CONV_CH)),
            _resident((qkv_rows, d)),
            _resident((HEAD_DIM, LANES)),
            _resident((HEAD_DIM, LANES)),
            pl.BlockSpec((HEAD_DIM, t), lambda i, j: (0, j)),
            pl.BlockSpec((HEAD_DIM, t), lambda i, j: (0, j)),
        ],
        out_specs=[
            pl.BlockSpec((1, t, CONV_CH), lambda i, j: (i, j, 0)),
            pl.BlockSpec((1, ATT_KV_HEADS, t // ATT_TQ, 2 * HEAD_DIM, ATT_N),
                         lambda i, j: (i, 0, j, 0, 0)),
            pl.BlockSpec((1, ATT_KV_HEADS, t // ATT_TQ, 1, ATT_N),
                         lambda i, j: (i, 0, j, 0, 0)),
            pl.BlockSpec((1, t, 2 * HEAD_DIM), lambda i, j: (i, j, 0)),
            pl.BlockSpec((1, ATT_KV_HEADS, 1, t), lambda i, j: (i, 0, 0, j)),
            pl.BlockSpec((1, ATT_KV_HEADS, t // ATT_TK, V_ROWS, ATT_TK),
                         lambda i, j: (i, 0, j, 0, 0)),
        ],
        out_shape=[
            jax.ShapeDtypeStruct((b, s, CONV_CH), F32),
            jax.ShapeDtypeStruct((b, ATT_KV_HEADS, s // ATT_TQ, 2 * HEAD_DIM, ATT_N), BF16),
            jax.ShapeDtypeStruct((b, ATT_KV_HEADS, s // ATT_TQ, 1, ATT_N), F32),
            jax.ShapeDtypeStruct((b, s, 2 * HEAD_DIM), BF16),
            jax.ShapeDtypeStruct((b, ATT_KV_HEADS, 1, s), F32),
            jax.ShapeDtypeStruct((b, ATT_KV_HEADS, s // ATT_TK, V_ROWS, ATT_TK), BF16),
        ],
        compiler_params=pltpu.CompilerParams(
            dimension_semantics=("arbitrary", "arbitrary"),
            vmem_limit_bytes=VMEM_LIMIT),
        name="in_proj",
    )(x, g, w_a, w_t, gq, gk, cos_t, sin_t)


ATT_MIN_DENOM = 2.0 ** -80


def _attn_kernel(qt_ref, qn_ref, k_ref, kn2_ref, vt_ref, o_ref, acc_ref, st_ref, m_ref):
    nkb = vt_ref.shape[2]

    def k_block(kb):
        return k_ref[0, pl.ds(pl.multiple_of(kb * ATT_TK, ATT_TK), ATT_TK), :]

    kmax = jnp.sqrt(jnp.max(kn2_ref[0, 0], axis=-1, keepdims=True))
    shift = qn_ref[0, 0, 0] * kmax

    def scores(kb, slot):
        st_ref[slot] = _dot(k_block(kb), qt_ref[0, 0, 0])

    def accumulate(kb, slot):
        p = jnp.exp2(st_ref[slot] - shift).astype(BF16)
        acc_ref[...] += _dot(vt_ref[0, 0, kb], p)

    acc_ref[...] = jnp.zeros(acc_ref.shape, F32)
    scores(0, 0)

    def group(i, carry):
        for u in range(ATT_UNROLL):
            kb = ATT_UNROLL * i + u
            scores(kb + 1, (u + 1) % 2)
            accumulate(kb, u % 2)
        return carry

    lax.fori_loop(0, nkb // ATT_UNROLL - 1, group, 0)
    for kb in range(nkb - ATT_UNROLL, nkb):
        if kb + 1 < nkb:
            scores(kb + 1, (kb + 1) % 2)
        accumulate(kb, kb % 2)

    denom_ok = jnp.min(acc_ref[HEAD_DIM:HEAD_DIM + 1, :]) >= ATT_MIN_DENOM

    @pl.when(jnp.logical_not(denom_ok))
    def _():
        m_ref[...] = jnp.full(m_ref.shape, -jnp.inf, F32)
        acc_ref[...] = jnp.zeros(acc_ref.shape, F32)

        def exact(kb, carry):
            st = _dot(k_block(kb), qt_ref[0, 0, 0])
            m_old = m_ref[...]
            m_new = jnp.maximum(m_old, jnp.max(st, axis=0, keepdims=True))
            p = jnp.exp2(st - m_new).astype(BF16)
            acc_ref[...] = acc_ref[...] * jnp.exp2(m_old - m_new) + _dot(vt_ref[0, 0, kb], p)
            m_ref[...] = m_new
            return carry

        lax.fori_loop(0, nkb, exact, 0)

    acc = acc_ref[...]
    outs = [acc[:HEAD_DIM, h * ATT_TQ:(h + 1) * ATT_TQ]
            / acc[HEAD_DIM:HEAD_DIM + 1, h * ATT_TQ:(h + 1) * ATT_TQ]
            for h in range(HEADS_PER_KV)]
    o_ref[0] = jnp.concatenate(outs, axis=0).T.astype(BF16)


def _attention(qt, qn, k, kn2, vt):
    b, s, _ = k.shape
    nkb = vt.shape[2]
    assert nkb % ATT_UNROLL == 0 and nkb >= 2 * ATT_UNROLL
    width = HEADS_PER_KV * HEAD_DIM
    return pl.pallas_call(
        _attn_kernel,
        grid=(b, ATT_KV_HEADS, s // ATT_TQ),
        in_specs=[
            pl.BlockSpec((1, 1, 1, 2 * HEAD_DIM, ATT_N), lambda i, j, q: (i, j, q, 0, 0)),
            pl.BlockSpec((1, 1, 1, 1, ATT_N), lambda i, j, q: (i, j, q, 0, 0)),
            pl.BlockSpec((1, s, 2 * HEAD_DIM), lambda i, j, q: (i, 0, 0)),
            pl.BlockSpec((1, 1, 1, s), lambda i, j, q: (i, j, 0, 0)),
            pl.BlockSpec((1, 1, nkb, V_ROWS, ATT_TK), lambda i, j, q: (i, j, 0, 0, 0)),
        ],
        out_specs=pl.BlockSpec((1, ATT_TQ, width), lambda i, j, q: (i, q, j)),
        out_shape=jax.ShapeDtypeStruct((b, s, ATT_KV_HEADS * width), BF16),
        scratch_shapes=[
            pltpu.VMEM((V_ROWS, ATT_N), F32),
            pltpu.VMEM((2, ATT_TK, ATT_N), F32),
            pltpu.VMEM((1, ATT_N), F32),
        ],
        compiler_params=pltpu.CompilerParams(
            dimension_semantics=("arbitrary", "arbitrary", "arbitrary"),
            vmem_limit_bytes=VMEM_LIMIT),
        name="attention",
    )(qt, qn, k, kn2, vt)


def _post_attn_kernel(x_ref, a_ref, ap_ref, an_ref, att_ref, dw_ref, dwb_ref,
                      lng_ref, lnb_ref, wout_ref, gpost_ref, gmem_ref, wq_ref,
                      kt_ref, v_ref, wo_ref, gmpost_ref, o_ref, abuf_ref, ash_ref, cbuf_ref):
    t = x_ref.shape[1]
    j = pl.program_id(1)
    nt = pl.num_programs(1)
    prev = jnp.where(j > 0, ap_ref[0], 0.0)
    nxt = jnp.where(j < nt - 1, an_ref[0], 0.0)
    abuf_ref[:CONV_HALO] = prev
    abuf_ref[CONV_HALO:CONV_HALO + t] = a_ref[0]
    abuf_ref[CONV_HALO + t:] = nxt
    nrows = t + 2 * CONV_HALO
    padded = abuf_ref[...]
    for r in range(1, SUBLANES):
        ash_ref[r - 1] = pltpu.roll(padded, nrows - r, axis=0)
    rows = 64
    base = CONV_HALO - CONV_PAD
    for r0 in range(0, t, rows):
        for c0 in range(0, CONV_CH, LANES):
            lanes = slice(c0, c0 + LANES)
            acc = jnp.broadcast_to(dwb_ref[:, lanes], (rows, LANES))
            for kk in range(CONV_K):
                al, r = divmod(base + kk, SUBLANES)
                lo = al * SUBLANES + r0
                src = abuf_ref if r == 0 else ash_ref.at[r - 1]
                acc = acc + dw_ref[kk:kk + 1, lanes] * src[lo:lo + rows, lanes]
            cbuf_ref[r0:r0 + rows, lanes] = acc
    c = cbuf_ref[...]
    mu = jnp.mean(c, axis=-1, keepdims=True)
    cc = c - mu
    var = jnp.mean(cc * cc, axis=-1, keepdims=True)
    y = cc * lax.rsqrt(var + EPS) * lng_ref[...] + lnb_ref[...]
    y = y * jax.nn.sigmoid(y)
    mixed = jnp.concatenate([y.astype(BF16), att_ref[0]], axis=-1)
    x1 = x_ref[0] + _rms(_dot(mixed, wout_ref[...]), gpost_ref[...])
    h = _rms(x1, gmem_ref[...]).astype(BF16)
    q = _dot(h, wq_ref[...]).astype(BF16)
    heads = []
    for hd in range(MEM_HEADS):
        sl = slice(hd * MEM_HEAD_DIM, (hd + 1) * MEM_HEAD_DIM)
        s = _dot(q[:, sl], kt_ref[0, sl, :])
        p = jnp.exp(s - jnp.max(s, axis=-1, keepdims=True))
        l = jnp.sum(p, axis=-1, keepdims=True)
        heads.append((_dot(p.astype(BF16), v_ref[0, :, sl]) / l).astype(BF16))
    o = jnp.concatenate(heads, axis=-1)
    o_ref[0] = x1 + _rms(_dot(o, wo_ref[...]), gmpost_ref[...])


def _post_attn(x, a, att, dw, dwb, lng, lnb, wout, gpost, gmem, wq, kt, v, wo, gmpost):
    b, s, d = x.shape
    t = TOK_TILE
    nt = s // t
    hb = t // CONV_HALO
    m = v.shape[1]
    vec = _resident((1, d))
    cvec = _resident((1, CONV_CH))
    return pl.pallas_call(
        _post_attn_kernel,
        grid=(b, nt),
        in_specs=[
            pl.BlockSpec((1, t, d), lambda i, j: (i, j, 0)),
            pl.BlockSpec((1, t, CONV_CH), lambda i, j: (i, j, 0)),
            pl.BlockSpec((1, CONV_HALO, CONV_CH),
                         lambda i, j: (i, jnp.maximum(j * hb - 1, 0), 0)),
            pl.BlockSpec((1, CONV_HALO, CONV_CH),
                         lambda i, j: (i, jnp.minimum((j + 1) * hb, nt * hb - 1), 0)),
            pl.BlockSpec((1, t, CONV_CH), lambda i, j: (i, j, 0)),
            _resident((CONV_K, CONV_CH)),
            cvec, cvec, cvec,
            _resident((d, d)),
            vec, vec,
            _resident((d, d)),
            pl.BlockSpec((1, d, m), lambda i, j: (i, 0, 0)),
            pl.BlockSpec((1, m, d), lambda i, j: (i, 0, 0)),
            _resident((d, d)),
            vec,
        ],
        out_specs=pl.BlockSpec((1, t, d), lambda i, j: (i, j, 0)),
        out_shape=jax.ShapeDtypeStruct((b, s, d), F32),
        scratch_shapes=[
            pltpu.VMEM((t + 2 * CONV_HALO, CONV_CH), F32),
            pltpu.VMEM((SUBLANES - 1, t + 2 * CONV_HALO, CONV_CH), F32),
            pltpu.VMEM((t, CONV_CH), F32),
        ],
        compiler_params=pltpu.CompilerParams(
            dimension_semantics=("arbitrary", "arbitrary"),
            vmem_limit_bytes=VMEM_LIMIT),
        name="post_attn",
    )(x, a, a, a, att, dw, dwb, lng, lnb, wout, gpost, gmem, wq, kt, v, wo, gmpost)


def _gelu_tanh(x):
    c = math.sqrt(2.0 / math.pi)
    half = 0.5 * x
    return half + half * jnp.tanh(x * (c + (c * 0.044715) * (x * x)))


def _ffn_kernel(x_ref, xp_ref, xn_ref, g_ref, wup_ref, dw_ref, dwb_ref, wdn_ref,
                gpost_ref, o_ref, hbuf_ref, hb_ref, acc_ref, u_ref):
    t = x_ref.shape[1]
    j = pl.program_id(1)
    nt = pl.num_programs(1)
    g = g_ref[...]
    hbuf_ref[:FFN_HALO] = jnp.where(j > 0, _rms(xp_ref[0], g), 0.0)
    hbuf_ref[FFN_HALO:FFN_HALO + t] = _rms(x_ref[0], g)
    hbuf_ref[FFN_HALO + t:] = jnp.where(j < nt - 1, _rms(xn_ref[0], g), 0.0)
    hb_ref[...] = hbuf_ref[...].astype(BF16)
    acc_ref[...] = jnp.zeros(acc_ref.shape, F32)
    rows = t + 2 * FFN_HALO
    d_ff = wdn_ref.shape[0]
    nc = d_ff // FFN_CHUNK

    def gate_val(ref, c):
        lo = c * FFN_CHUNK
        return jnp.concatenate([ref[:, lo:lo + FFN_CHUNK],
                                ref[:, d_ff + lo:d_ff + lo + FFN_CHUNK]], axis=1)

    def up(c, slot):
        u_ref[slot] = _dot(hb_ref[...], gate_val(wup_ref, c))

    def down(c, slot):
        u = u_ref[slot]
        dw = gate_val(dw_ref, c)
        prev = pltpu.roll(u, 1, axis=0)
        nxt = pltpu.roll(u, rows - 1, axis=0)
        mid = slice(FFN_HALO, FFN_HALO + t)
        conv = (dw[0:1] * prev[mid] + dw[1:2] * u[mid] + dw[2:3] * nxt[mid]
                + gate_val(dwb_ref, c))
        act = _gelu_tanh(conv[:, :FFN_CHUNK]) * conv[:, FFN_CHUNK:]
        acc_ref[...] += _dot(act.astype(BF16),
                             wdn_ref[c * FFN_CHUNK:(c + 1) * FFN_CHUNK, :])

    up(0, 0)
    for c in range(nc):
        if c + 1 < nc:
            up(c + 1, (c + 1) % 2)
        down(c, c % 2)
    o_ref[0] = x_ref[0] + _rms(acc_ref[...], gpost_ref[...])


def _ffn(x, g, wup, dw, dwb, wdn, gpost):
    b, s, d = x.shape
    t = FFN_TILE
    nt = s // t
    hb = t // FFN_HALO
    d_ff = wdn.shape[0]
    return pl.pallas_call(
        _ffn_kernel,
        grid=(b, nt),
        in_specs=[
            pl.BlockSpec((1, t, d), lambda i, j: (i, j, 0)),
            pl.BlockSpec((1, FFN_HALO, d),
                         lambda i, j: (i, jnp.maximum(j * hb - 1, 0), 0)),
            pl.BlockSpec((1, FFN_HALO, d),
                         lambda i, j: (i, jnp.minimum((j + 1) * hb, nt * hb - 1), 0)),
            _resident((1, d)),
            _resident((d, 2 * d_ff)),
            _resident((3, 2 * d_ff)),
            _resident((1, 2 * d_ff)),
            _resident((d_ff, d)),
            _resident((1, d)),
        ],
        out_specs=pl.BlockSpec((1, t, d), lambda i, j: (i, j, 0)),
        out_shape=jax.ShapeDtypeStruct((b, s, d), F32),
        scratch_shapes=[
            pltpu.VMEM((t + 2 * FFN_HALO, d), F32),
            pltpu.VMEM((t + 2 * FFN_HALO, d), BF16),
            pltpu.VMEM((t, d), F32),
            pltpu.VMEM((2, t + 2 * FFN_HALO, 2 * FFN_CHUNK), F32),
        ],
        compiler_params=pltpu.CompilerParams(
            dimension_semantics=("arbitrary", "arbitrary"),
            vmem_limit_bytes=VMEM_LIMIT),
        name="ffn",
    )(x, x, x, g, wup, dw, dwb, wdn, gpost)


def _rope_tables_t(s):
    f32 = np.float32
    rows = s // GRID_W
    r = np.repeat(np.arange(rows, dtype=f32), GRID_W)
    c = np.tile(np.arange(GRID_W, dtype=f32), rows)
    axis_dim = HEAD_DIM // 2
    inv = f32(ROPE_THETA) ** (-np.arange(0, axis_dim, 2, dtype=f32) / f32(axis_dim))
    ang_r = (r[:, None] * inv[None, :]).T.astype(f32)
    ang_c = (c[:, None] * inv[None, :]).T.astype(f32)
    cos_t = np.concatenate([np.cos(ang_r)] * 2 + [np.cos(ang_c)] * 2, axis=0)
    sr, sc = np.sin(ang_r), np.sin(ang_c)
    sin_t = np.concatenate([-sr, sr, -sc, sc], axis=0)
    return jnp.asarray(cos_t, F32), jnp.asarray(sin_t, F32)


def kernel(x, mem, norm_mix_pre, w_in, conv_dw, conv_dw_b, conv_ln_g, conv_ln_b,
           q_norm_g, k_norm_g, w_out, norm_mix_post, norm_mem_pre, mem_norm_g,
           w_mem_q, w_mem_kv, w_mem_o, norm_mem_post, norm_ffn_pre, w_up, ffn_dw,
           ffn_dw_b, w_down, norm_ffn_post):
    depth = w_in.shape[0]
    s = x.shape[1]
    assert s % TOK_TILE == 0 and s % ATT_TK == 0 and TOK_TILE % ATT_TK == 0
    assert w_down.shape[1] % FFN_CHUNK == 0
    cos_t, sin_t = _rope_tables_t(s)
    row = lambda v: v.reshape(1, -1)
    col = lambda v: jnp.broadcast_to(v[:, None], (v.shape[0], LANES))
    for l in range(depth):
        glu_cols = 2 * CONV_CH
        w_a = w_in[l, :, :glu_cols].astype(BF16)
        w_t = w_in[l, :, glu_cols:].T.astype(BF16)
        a, qt, qn, k, kn2, vt = _in_proj(x, row(norm_mix_pre[l]), w_a, w_t,
                                         col(q_norm_g[l]), col(k_norm_g[l]), cos_t, sin_t)
        att = _attention(qt, qn, k, kn2, vt)
        kt_mem, v_mem = _mem_kv(mem, row(mem_norm_g[l]), w_mem_kv[l].astype(BF16))
        x = _post_attn(x, a, att, conv_dw[l], row(conv_dw_b[l]), row(conv_ln_g[l]),
                       row(conv_ln_b[l]), w_out[l].astype(BF16), row(norm_mix_post[l]),
                       row(norm_mem_pre[l]), w_mem_q[l].astype(BF16), kt_mem, v_mem,
                       w_mem_o[l].astype(BF16), row(norm_mem_post[l]))
        x = _ffn(x, row(norm_ffn_pre[l]), w_up[l].astype(BF16), ffn_dw[l],
                 row(ffn_dw_b[l]), w_down[l].astype(BF16), row(norm_ffn_post[l]))
    return x
```

```python
import math

import jax
import jax.numpy as jnp
import numpy as np
from jax import lax
from jax.experimental import pallas as pl
from jax.experimental.pallas import tpu as pltpu

F32 = jnp.float32
BF16 = jnp.bfloat16

GRID_W = 64
HEAD_DIM = 64
ATT_HEADS = 8
ATT_KV_HEADS = 2
HEADS_PER_KV = ATT_HEADS // ATT_KV_HEADS
CONV_CH = 512
CONV_K = 31
CONV_PAD = (CONV_K - 1) // 2
MEM_HEADS = 4
MEM_HEAD_DIM = 256
ROPE_THETA = 10000.0
EPS = 1e-6
LOG2E = 1.4426950408889634

LANES = 128
SUBLANES = 8
BF16_ROWS = 16
VMEM_LIMIT = 56 * 1024 * 1024

TOK_TILE = 512
FFN_TILE = 256
ATT_TQ = 256
ATT_N = HEADS_PER_KV * ATT_TQ
ATT_TK = 256
ATT_UNROLL = 8
V_ROWS = HEAD_DIM + BF16_ROWS
CONV_HALO = 16
FFN_HALO = 8
FFN_CHUNK = 256


def _rms(xf, g):
    return xf * lax.rsqrt(jnp.mean(xf * xf, axis=-1, keepdims=True) + EPS) * g


def _dot(a, b):
    return jnp.dot(a, b, preferred_element_type=F32)


def _resident(shape):
    zeros = (0,) * len(shape)
    return pl.BlockSpec(shape, lambda *_: zeros, pipeline_mode=pl.Buffered(1))


def _mem_kv_kernel(mem_ref, g_ref, w_ref, kt_ref, v_ref):
    d = mem_ref.shape[-1]
    mn = _rms(mem_ref[0], g_ref[...]).astype(BF16)
    kv = _dot(mn, w_ref[...])
    kt_ref[0] = (kv[:, :d] * (MEM_HEAD_DIM ** -0.5)).T.astype(BF16)
    v_ref[0] = kv[:, d:].astype(BF16)


def _mem_kv(mem, g, w_kv):
    b, m, d = mem.shape
    return pl.pallas_call(
        _mem_kv_kernel,
        grid=(b,),
        in_specs=[
            pl.BlockSpec((1, m, d), lambda i: (i, 0, 0)),
            _resident((1, d)),
            _resident((d, 2 * d)),
        ],
        out_specs=[
            pl.BlockSpec((1, d, m), lambda i: (i, 0, 0)),
            pl.BlockSpec((1, m, d), lambda i: (i, 0, 0)),
        ],
        out_shape=[
            jax.ShapeDtypeStruct((b, d, m), BF16),
            jax.ShapeDtypeStruct((b, m, d), BF16),
        ],
        compiler_params=pltpu.CompilerParams(
            dimension_semantics=("arbitrary",), vmem_limit_bytes=VMEM_LIMIT),
        name="mem_kv",
    )(mem, g, w_kv)


def _rope_partner(t):
    q = HEAD_DIM // 4
    return jnp.concatenate([t[q:2 * q], t[:q], t[3 * q:], t[2 * q:3 * q]], axis=0)


def _norm_rope_t(zt, g, cos_t, sin_t):
    r = lax.rsqrt(jnp.mean(zt * zt, axis=0, keepdims=True) + EPS)
    y = zt * r * g
    return y * cos_t + _rope_partner(y) * sin_t


def _col_norm2(t_bf16):
    tf = t_bf16.astype(F32)
    return jnp.sum(tf * tf, axis=0, keepdims=True)


def _in_proj_kernel(x_ref, g_ref, wa_ref, wt_ref, gq_ref, gk_ref, cos_ref, sin_ref,
                    a_ref, qt_ref, qn_ref, k_ref, kn2_ref, vt_ref):
    t = x_ref.shape[1]
    h = _rms(x_ref[0], g_ref[...]).astype(BF16)
    za = _dot(h, wa_ref[...])
    a_ref[0] = za[:, :CONV_CH] * jax.nn.sigmoid(za[:, CONV_CH:])
    zt = lax.dot_general(wt_ref[...], h, (((1,), (1,)), ((), ())),
                         preferred_element_type=F32)
    reps = t // LANES
    gq = jnp.concatenate([gq_ref[...]] * reps, axis=1)
    gk = jnp.concatenate([gk_ref[...]] * reps, axis=1)
    cos_t = cos_ref[...]
    sin_t = sin_ref[...]
    qscale = (HEAD_DIM ** -0.5) * LOG2E
    zero = jnp.zeros((HEAD_DIM, ATT_TQ), BF16)
    for hd in range(ATT_HEADS):
        q = _norm_rope_t(zt[hd * HEAD_DIM:(hd + 1) * HEAD_DIM], gq, cos_t, sin_t)
        q = (q * qscale).astype(BF16)
        qn = jnp.sqrt(_col_norm2(q))
        kv, hh = divmod(hd, HEADS_PER_KV)
        cols = slice(hh * ATT_TQ, (hh + 1) * ATT_TQ)
        for c in range(t // ATT_TQ):
            toks = slice(c * ATT_TQ, (c + 1) * ATT_TQ)
            for j in range(ATT_KV_HEADS):
                qt_ref[0, kv, c, j * HEAD_DIM:(j + 1) * HEAD_DIM, cols] = (
                    q[:, toks] if j == kv else zero)
            qn_ref[0, kv, c, :, cols] = qn[:, toks]
    o = ATT_HEADS * HEAD_DIM
    ks = [_norm_rope_t(zt[o + j * HEAD_DIM:o + (j + 1) * HEAD_DIM], gk, cos_t, sin_t)
          for j in range(ATT_KV_HEADS)]
    for j in range(ATT_KV_HEADS):
        kn2_ref[0, j] = _col_norm2(ks[j].astype(BF16))
    k_ref[0] = jnp.concatenate(ks, axis=0).T.astype(BF16)
    o += ATT_KV_HEADS * HEAD_DIM
    row = lax.broadcasted_iota(jnp.int32, (BF16_ROWS, ATT_TK), 0)
    ones_rows = jnp.where(row == 0, 1.0, 0.0).astype(BF16)
    for j in range(ATT_KV_HEADS):
        v = zt[o + j * HEAD_DIM:o + (j + 1) * HEAD_DIM].astype(BF16)
        for c in range(t // ATT_TK):
            vt_ref[0, j, c, :HEAD_DIM, :] = v[:, c * ATT_TK:(c + 1) * ATT_TK]
            vt_ref[0, j, c, HEAD_DIM:, :] = ones_rows


def _in_proj(x, g, w_a, w_t, gq, gk, cos_t, sin_t):
    b, s, d = x.shape
    t = TOK_TILE
    nt = s // t
    qkv_rows = w_t.shape[0]
    return pl.pallas_call(
        _in_proj_kernel,
        grid=(b, nt),
        in_specs=[
            pl.BlockSpec((1, t, d), lambda i, j: (i, j, 0)),
            _resident((1, d)),
            _resident((d, 2 * CONV_CH)),
            _resident((qkv_rows, d)),
            _resident((HEAD_DIM, LANES)),
            _resident((HEAD_DIM, LANES)),
            pl.BlockSpec((HEAD_DIM, t), lambda i, j: (0, j)),
            pl.BlockSpec((HEAD_DIM, t), lambda i, j: (0, j)),
        ],
        out_specs=[
            pl.BlockSpec((1, t, CONV_CH), lambda i, j: (i, j, 0)),
            pl.BlockSpec((1, ATT_KV_HEADS, t // ATT_TQ, 2 * HEAD_DIM, ATT_N),
                         lambda i, j: (i, 0, j, 0, 0)),
            pl.BlockSpec((1, ATT_KV_HEADS, t // ATT_TQ, 1, ATT_N),
                         lambda i, j: (i, 0, j, 0, 0)),
            pl.BlockSpec((1, t, 2 * HEAD_DIM), lambda i, j: (i, j, 0)),
            pl.BlockSpec((1, ATT_KV_HEADS, 1, t), lambda i, j: (i, 0, 0, j)),
            pl.BlockSpec((1, ATT_KV_HEADS, t // ATT_TK, V_ROWS, ATT_TK),
                         lambda i, j: (i, 0, j, 0, 0)),
        ],
        out_shape=[
            jax.ShapeDtypeStruct((b, s, CONV_CH), F32),
            jax.ShapeDtypeStruct((b, ATT_KV_HEADS, s // ATT_TQ, 2 * HEAD_DIM, ATT_N), BF16),
            jax.ShapeDtypeStruct((b, ATT_KV_HEADS, s // ATT_TQ, 1, ATT_N), F32),
            jax.ShapeDtypeStruct((b, s, 2 * HEAD_DIM), BF16),
            jax.ShapeDtypeStruct((b, ATT_KV_HEADS, 1, s), F32),
            jax.ShapeDtypeStruct((b, ATT_KV_HEADS, s // ATT_TK, V_ROWS, ATT_TK), BF16),
        ],
        compiler_params=pltpu.CompilerParams(
            dimension_semantics=("arbitrary", "arbitrary"),
            vmem_limit_bytes=VMEM_LIMIT),
        name="in_proj",
    )(x, g, w_a, w_t, gq, gk, cos_t, sin_t)


ATT_MIN_DENOM = 2.0 ** -80


def _attn_kernel(qt_ref, qn_ref, k_ref, kn2_ref, vt_ref, o_ref, acc_ref, st_ref, m_ref):
    nq = qt_ref.shape[2]
    nkb = vt_ref.shape[2]

    def k_block(kb):
        return k_ref[0, pl.ds(pl.multiple_of(kb * ATT_TK, ATT_TK), ATT_TK), :]

    kmax = jnp.sqrt(jnp.max(kn2_ref[0, 0], axis=-1, keepdims=True))

    def scores(qb, kb, slot):
        st_ref[slot] = _dot(k_block(kb), qt_ref[0, 0, qb])

    def accumulate(kb, slot, shift):
        p = jnp.exp2(st_ref[slot] - shift).astype(BF16)
        acc_ref[...] += _dot(vt_ref[0, 0, kb], p)

    def exact_block(qb):
        m_ref[...] = jnp.full(m_ref.shape, -jnp.inf, F32)
        acc_ref[...] = jnp.zeros(acc_ref.shape, F32)

        def exact(kb, carry):
            st = _dot(k_block(kb), qt_ref[0, 0, qb])
            m_old = m_ref[...]
            m_new = jnp.maximum(m_old, jnp.max(st, axis=0, keepdims=True))
            p = jnp.exp2(st - m_new).astype(BF16)
            acc_ref[...] = acc_ref[...] * jnp.exp2(m_old - m_new) + _dot(vt_ref[0, 0, kb], p)
            m_ref[...] = m_new
            return carry

        lax.fori_loop(0, nkb, exact, 0)

    scores(0, 0, 0)

    def query_block(qb, carry):
        shift = qn_ref[0, 0, qb] * kmax
        acc_ref[...] = jnp.zeros(acc_ref.shape, F32)

        def group(i, c):
            for u in range(ATT_UNROLL):
                kb = ATT_UNROLL * i + u
                scores(qb, kb + 1, (u + 1) % 2)
                accumulate(kb, u % 2, shift)
            return c

        lax.fori_loop(0, nkb // ATT_UNROLL - 1, group, 0)
        for kb in range(nkb - ATT_UNROLL, nkb):
            if kb + 1 < nkb:
                scores(qb, kb + 1, (kb + 1) % 2)
            else:
                scores(jnp.minimum(qb + 1, nq - 1), 0, 0)
            accumulate(kb, kb % 2, shift)

        denom_ok = jnp.min(acc_ref[HEAD_DIM:HEAD_DIM + 1, :]) >= ATT_MIN_DENOM
        pl.when(jnp.logical_not(denom_ok))(lambda: exact_block(qb))

        acc = acc_ref[...]
        outs = [acc[:HEAD_DIM, h * ATT_TQ:(h + 1) * ATT_TQ]
                / acc[HEAD_DIM:HEAD_DIM + 1, h * ATT_TQ:(h + 1) * ATT_TQ]
                for h in range(HEADS_PER_KV)]
        rows = pl.ds(pl.multiple_of(qb * ATT_TQ, ATT_TQ), ATT_TQ)
        o_ref[0, rows, :] = jnp.concatenate(outs, axis=0).T.astype(BF16)
        return carry

    lax.fori_loop(0, nq, query_block, 0)


def _attention(qt, qn, k, kn2, vt):
    b, s, _ = k.shape
    nq = qt.shape[2]
    nkb = vt.shape[2]
    assert nkb % ATT_UNROLL == 0 and nkb >= 2 * ATT_UNROLL and ATT_UNROLL % 2 == 0
    width = HEADS_PER_KV * HEAD_DIM
    return pl.pallas_call(
        _attn_kernel,
        grid=(b, ATT_KV_HEADS),
        in_specs=[
            pl.BlockSpec((1, 1, nq, 2 * HEAD_DIM, ATT_N), lambda i, j: (i, j, 0, 0, 0)),
            pl.BlockSpec((1, 1, nq, 1, ATT_N), lambda i, j: (i, j, 0, 0, 0)),
            pl.BlockSpec((1, s, 2 * HEAD_DIM), lambda i, j: (i, 0, 0)),
            pl.BlockSpec((1, 1, 1, s), lambda i, j: (i, j, 0, 0)),
            pl.BlockSpec((1, 1, nkb, V_ROWS, ATT_TK), lambda i, j: (i, j, 0, 0, 0)),
        ],
        out_specs=pl.BlockSpec((1, s, width), lambda i, j: (i, 0, j)),
        out_shape=jax.ShapeDtypeStruct((b, s, ATT_KV_HEADS * width), BF16),
        scratch_shapes=[
            pltpu.VMEM((V_ROWS, ATT_N), F32),
            pltpu.VMEM((2, ATT_TK, ATT_N), F32),
            pltpu.VMEM((1, ATT_N), F32),
        ],
        compiler_params=pltpu.CompilerParams(
            dimension_semantics=("arbitrary", "arbitrary"),
            vmem_limit_bytes=VMEM_LIMIT),
        name="attention",
    )(qt, qn, k, kn2, vt)


def _post_attn_kernel(x_ref, a_ref, ap_ref, an_ref, att_ref, dw_ref, dwb_ref,
                      lng_ref, lnb_ref, wout_ref, gpost_ref, gmem_ref, wq_ref,
                      kt_ref, v_ref, wo_ref, gmpost_ref, o_ref, abuf_ref, ash_ref, cbuf_ref):
    t = x_ref.shape[1]
    j = pl.program_id(1)
    nt = pl.num_programs(1)
    prev = jnp.where(j > 0, ap_ref[0], 0.0)
    nxt = jnp.where(j < nt - 1, an_ref[0], 0.0)
    abuf_ref[:CONV_HALO] = prev
    abuf_ref[CONV_HALO:CONV_HALO + t] = a_ref[0]
    abuf_ref[CONV_HALO + t:] = nxt
    nrows = t + 2 * CONV_HALO
    padded = abuf_ref[...]
    for r in range(1, SUBLANES):
        ash_ref[r - 1] = pltpu.roll(padded, nrows - r, axis=0)
    rows = 64
    base = CONV_HALO - CONV_PAD
    for r0 in range(0, t, rows):
        for c0 in range(0, CONV_CH, LANES):
            lanes = slice(c0, c0 + LANES)
            acc = jnp.broadcast_to(dwb_ref[:, lanes], (rows, LANES))
            for kk in range(CONV_K):
                al, r = divmod(base + kk, SUBLANES)
                lo = al * SUBLANES + r0
                src = abuf_ref if r == 0 else ash_ref.at[r - 1]
                acc = acc + dw_ref[kk:kk + 1, lanes] * src[lo:lo + rows, lanes]
            cbuf_ref[r0:r0 + rows, lanes] = acc
    c = cbuf_ref[...]
    mu = jnp.mean(c, axis=-1, keepdims=True)
    cc = c - mu
    var = jnp.mean(cc * cc, axis=-1, keepdims=True)
    y = cc * lax.rsqrt(var + EPS) * lng_ref[...] + lnb_ref[...]
    y = y * jax.nn.sigmoid(y)
    mixed = jnp.concatenate([y.astype(BF16), att_ref[0]], axis=-1)
    x1 = x_ref[0] + _rms(_dot(mixed, wout_ref[...]), gpost_ref[...])
    h = _rms(x1, gmem_ref[...]).astype(BF16)
    q = _dot(h, wq_ref[...]).astype(BF16)
    heads = []
    for hd in range(MEM_HEADS):
        sl = slice(hd * MEM_HEAD_DIM, (hd + 1) * MEM_HEAD_DIM)
        s = _dot(q[:, sl], kt_ref[0, sl, :])
        p = jnp.exp(s - jnp.max(s, axis=-1, keepdims=True))
        l = jnp.sum(p, axis=-1, keepdims=True)
        heads.append((_dot(p.astype(BF16), v_ref[0, :, sl]) / l).astype(BF16))
    o = jnp.concatenate(heads, axis=-1)
    o_ref[0] = x1 + _rms(_dot(o, wo_ref[...]), gmpost_ref[...])


def _post_attn(x, a, att, dw, dwb, lng, lnb, wout, gpost, gmem, wq, kt, v, wo, gmpost):
    b, s, d = x.shape
    t = TOK_TILE
    nt = s // t
    hb = t // CONV_HALO
    m = v.shape[1]
    vec = _resident((1, d))
    cvec = _resident((1, CONV_CH))
    return pl.pallas_call(
        _post_attn_kernel,
        grid=(b, nt),
        in_specs=[
            pl.BlockSpec((1, t, d), lambda i, j: (i, j, 0)),
            pl.BlockSpec((1, t, CONV_CH), lambda i, j: (i, j, 0)),
            pl.BlockSpec((1, CONV_HALO, CONV_CH),
                         lambda i, j: (i, jnp.maximum(j * hb - 1, 0), 0)),
            pl.BlockSpec((1, CONV_HALO, CONV_CH),
                         lambda i, j: (i, jnp.minimum((j + 1) * hb, nt * hb - 1), 0)),
            pl.BlockSpec((1, t, CONV_CH), lambda i, j: (i, j, 0)),
            _resident((CONV_K, CONV_CH)),
            cvec, cvec, cvec,
            _resident((d, d)),
            vec, vec,
            _resident((d, d)),
            pl.BlockSpec((1, d, m), lambda i, j: (i, 0, 0)),
            pl.BlockSpec((1, m, d), lambda i, j: (i, 0, 0)),
            _resident((d, d)),
            vec,
        ],
        out_specs=pl.BlockSpec((1, t, d), lambda i, j: (i, j, 0)),
        out_shape=jax.ShapeDtypeStruct((b, s, d), F32),
        scratch_shapes=[
            pltpu.VMEM((t + 2 * CONV_HALO, CONV_CH), F32),
            pltpu.VMEM((SUBLANES - 1, t + 2 * CONV_HALO, CONV_CH), F32),
            pltpu.VMEM((t, CONV_CH), F32),
        ],
        compiler_params=pltpu.CompilerParams(
            dimension_semantics=("arbitrary", "arbitrary"),
            vmem_limit_bytes=VMEM_LIMIT),
        name="post_attn",
    )(x, a, a, a, att, dw, dwb, lng, lnb, wout, gpost, gmem, wq, kt, v, wo, gmpost)


def _gelu_tanh(x):
    c = math.sqrt(2.0 / math.pi)
    half = 0.5 * x
    return half + half * jnp.tanh(x * (c + (c * 0.044715) * (x * x)))


def _ffn_kernel(x_ref, xp_ref, xn_ref, g_ref, wup_ref, dw_ref, dwb_ref, wdn_ref,
                gpost_ref, o_ref, hbuf_ref, hb_ref, acc_ref, u_ref):
    t = x_ref.shape[1]
    j = pl.program_id(1)
    nt = pl.num_programs(1)
    g = g_ref[...]
    hbuf_ref[:FFN_HALO] = jnp.where(j > 0, _rms(xp_ref[0], g), 0.0)
    hbuf_ref[FFN_HALO:FFN_HALO + t] = _rms(x_ref[0], g)
    hbuf_ref[FFN_HALO + t:] = jnp.where(j < nt - 1, _rms(xn_ref[0], g), 0.0)
    hb_ref[...] = hbuf_ref[...].astype(BF16)
    acc_ref[...] = jnp.zeros(acc_ref.shape, F32)
    rows = t + 2 * FFN_HALO
    d_ff = wdn_ref.shape[0]
    nc = d_ff // FFN_CHUNK

    def gate_val(ref, c):
        lo = c * FFN_CHUNK
        return jnp.concatenate([ref[:, lo:lo + FFN_CHUNK],
                                ref[:, d_ff + lo:d_ff + lo + FFN_CHUNK]], axis=1)

    def up(c, slot):
        u_ref[slot] = _dot(hb_ref[...], gate_val(wup_ref, c))

    def down(c, slot):
        u = u_ref[slot]
        dw = gate_val(dw_ref, c)
        prev = pltpu.roll(u, 1, axis=0)
        nxt = pltpu.roll(u, rows - 1, axis=0)
        mid = slice(FFN_HALO, FFN_HALO + t)
        conv = (dw[0:1] * prev[mid] + dw[1:2] * u[mid] + dw[2:3] * nxt[mid]
                + gate_val(dwb_ref, c))
        act = _gelu_tanh(conv[:, :FFN_CHUNK]) * conv[:, FFN_CHUNK:]
        acc_ref[...] += _dot(act.astype(BF16),
                             wdn_ref[c * FFN_CHUNK:(c + 1) * FFN_CHUNK, :])

    up(0, 0)
    for c in range(nc):
        if c + 1 < nc:
            up(c + 1, (c + 1) % 2)
        down(c, c % 2)
    o_ref[0] = x_ref[0] + _rms(acc_ref[...], gpost_ref[...])


def _ffn(x, g, wup, dw, dwb, wdn, gpost):
    b, s, d = x.shape
    t = FFN_TILE
    nt = s // t
    hb = t // FFN_HALO
    d_ff = wdn.shape[0]
    return pl.pallas_call(
        _ffn_kernel,
        grid=(b, nt),
        in_specs=[
            pl.BlockSpec((1, t, d), lambda i, j: (i, j, 0)),
            pl.BlockSpec((1, FFN_HALO, d),
                         lambda i, j: (i, jnp.maximum(j * hb - 1, 0), 0)),
            pl.BlockSpec((1, FFN_HALO, d),
                         lambda i, j: (i, jnp.minimum((j + 1) * hb, nt * hb - 1), 0)),
            _resident((1, d)),
            _resident((d, 2 * d_ff)),
            _resident((3, 2 * d_ff)),
            _resident((1, 2 * d_ff)),
            _resident((d_ff, d)),
            _resident((1, d)),
        ],
        out_specs=pl.BlockSpec((1, t, d), lambda i, j: (i, j, 0)),
        out_shape=jax.ShapeDtypeStruct((b, s, d), F32),
        scratch_shapes=[
            pltpu.VMEM((t + 2 * FFN_HALO, d), F32),
            pltpu.VMEM((t + 2 * FFN_HALO, d), BF16),
            pltpu.VMEM((t, d), F32),
            pltpu.VMEM((2, t + 2 * FFN_HALO, 2 * FFN_CHUNK), F32),
        ],
        compiler_params=pltpu.CompilerParams(
            dimension_semantics=("arbitrary", "arbitrary"),
            vmem_limit_bytes=VMEM_LIMIT),
        name="ffn",
    )(x, x, x, g, wup, dw, dwb, wdn, gpost)


def _rope_tables_t(s):
    f32 = np.float32
    rows = s // GRID_W
    r = np.repeat(np.arange(rows, dtype=f32), GRID_W)
    c = np.tile(np.arange(GRID_W, dtype=f32), rows)
    axis_dim = HEAD_DIM // 2
    inv = f32(ROPE_THETA) ** (-np.arange(0, axis_dim, 2, dtype=f32) / f32(axis_dim))
    ang_r = (r[:, None] * inv[None, :]).T.astype(f32)
    ang_c = (c[:, None] * inv[None, :]).T.astype(f32)
    cos_t = np.concatenate([np.cos(ang_r)] * 2 + [np.cos(ang_c)] * 2, axis=0)
    sr, sc = np.sin(ang_r), np.sin(ang_c)
    sin_t = np.concatenate([-sr, sr, -sc, sc], axis=0)
    return jnp.asarray(cos_t, F32), jnp.asarray(sin_t, F32)


def kernel(x, mem, norm_mix_pre, w_in, conv_dw, conv_dw_b, conv_ln_g, conv_ln_b,
           q_norm_g, k_norm_g, w_out, norm_mix_post, norm_mem_pre, mem_norm_g,
           w_mem_q, w_mem_kv, w_mem_o, norm_mem_post, norm_ffn_pre, w_up, ffn_dw,
           ffn_dw_b, w_down, norm_ffn_post):
    depth = w_in.shape[0]
    s = x.shape[1]
    assert s % TOK_TILE == 0 and s % ATT_TK == 0 and TOK_TILE % ATT_TK == 0
    assert w_down.shape[1] % FFN_CHUNK == 0
    cos_t, sin_t = _rope_tables_t(s)
    row = lambda v: v.reshape(1, -1)
    col = lambda v: jnp.broadcast_to(v[:, None], (v.shape[0], LANES))
    for l in range(depth):
        glu_cols = 2 * CONV_CH
        w_a = w_in[l, :, :glu_cols].astype(BF16)
        w_t = w_in[l, :, glu_cols:].T.astype(BF16)
        a, qt, qn, k, kn2, vt = _in_proj(x, row(norm_mix_pre[l]), w_a, w_t,
                                         col(q_norm_g[l]), col(k_norm_g[l]), cos_t, sin_t)
        att = _attention(qt, qn, k, kn2, vt)
        kt_mem, v_mem = _mem_kv(mem, row(mem_norm_g[l]), w_mem_kv[l].astype(BF16))
        x = _post_attn(x, a, att, conv_dw[l], row(conv_dw_b[l]), row(conv_ln_g[l]),
                       row(conv_ln_b[l]), w_out[l].astype(BF16), row(norm_mix_post[l]),
                       row(norm_mem_pre[l]), w_mem_q[l].astype(BF16), kt_mem, v_mem,
                       w_mem_o[l].astype(BF16), row(norm_mem_post[l]))
        x = _ffn(x, row(norm_ffn_pre[l]), w_up[l].astype(BF16), ffn_dw[l],
                 row(ffn_dw_b[l]), w_down[l].astype(BF16), row(norm_ffn_post[l]))
    return x
```

```python
import math

import jax
import jax.numpy as jnp
import numpy as np
from jax import lax
from jax.experimental import pallas as pl
from jax.experimental.pallas import tpu as pltpu

F32 = jnp.float32
BF16 = jnp.bfloat16

GRID_W = 64
HEAD_DIM = 64
ATT_HEADS = 8
ATT_KV_HEADS = 2
HEADS_PER_KV = ATT_HEADS // ATT_KV_HEADS
CONV_CH = 512
CONV_K = 31
CONV_PAD = (CONV_K - 1) // 2
MEM_HEADS = 4
MEM_HEAD_DIM = 256
ROPE_THETA = 10000.0
EPS = 1e-6
LOG2E = 1.4426950408889634

LANES = 128
SUBLANES = 8
BF16_ROWS = 16
VMEM_LIMIT = 56 * 1024 * 1024

TOK_TILE = 512
FFN_TILE = 256
ATT_TQ = 256
ATT_N = HEADS_PER_KV * ATT_TQ
ATT_TK = 256
ATT_UNROLL = 16
ATT_FINISH_AT = 4
V_ROWS =HEAD_DIM + BF16_ROWS
CONV_HALO = 16
FFN_HALO = 8
FFN_CHUNK = 256


def _rms(xf, g):
    return xf * lax.rsqrt(jnp.mean(xf * xf, axis=-1, keepdims=True) + EPS) * g


def _dot(a, b):
    return jnp.dot(a, b, preferred_element_type=F32)


def _resident(shape):
    zeros = (0,) * len(shape)
    return pl.BlockSpec(shape, lambda *_: zeros, pipeline_mode=pl.Buffered(1))


def _mem_kv_kernel(mem_ref, g_ref, w_ref, kt_ref, v_ref):
    d = mem_ref.shape[-1]
    mn = _rms(mem_ref[0], g_ref[...]).astype(BF16)
    kv = _dot(mn, w_ref[...])
    kt_ref[0] = (kv[:, :d] * (MEM_HEAD_DIM ** -0.5)).T.astype(BF16)
    v_ref[0] = kv[:, d:].astype(BF16)


def _mem_kv(mem, g, w_kv):
    b, m, d = mem.shape
    return pl.pallas_call(
        _mem_kv_kernel,
        grid=(b,),
        in_specs=[
            pl.BlockSpec((1, m, d), lambda i: (i, 0, 0)),
            _resident((1, d)),
            _resident((d, 2 * d)),
        ],
        out_specs=[
            pl.BlockSpec((1, d, m), lambda i: (i, 0, 0)),
            pl.BlockSpec((1, m, d), lambda i: (i, 0, 0)),
        ],
        out_shape=[
            jax.ShapeDtypeStruct((b, d, m), BF16),
            jax.ShapeDtypeStruct((b, m, d), BF16),
        ],
        compiler_params=pltpu.CompilerParams(
            dimension_semantics=("arbitrary",), vmem_limit_bytes=VMEM_LIMIT),
        name="mem_kv",
    )(mem, g, w_kv)


def _rope_partner(t):
    q = HEAD_DIM // 4
    return jnp.concatenate([t[q:2 * q], t[:q], t[3 * q:], t[2 * q:3 * q]], axis=0)


def _norm_rope_t(zt, g, cos_t, sin_t):
    r = lax.rsqrt(jnp.mean(zt * zt, axis=0, keepdims=True) + EPS)
    y = zt * r * g
    return y * cos_t + _rope_partner(y) * sin_t


def _col_norm2(t_bf16):
    tf = t_bf16.astype(F32)
    return jnp.sum(tf * tf, axis=0, keepdims=True)


def _in_proj_kernel(x_ref, g_ref, wa_ref, wt_ref, gq_ref, gk_ref, cos_ref, sin_ref,
                    a_ref, qt_ref, qn_ref, k_ref, kn2_ref, vt_ref):
    t = x_ref.shape[1]
    h = _rms(x_ref[0], g_ref[...]).astype(BF16)
    za = _dot(h, wa_ref[...])
    a_ref[0] = za[:, :CONV_CH] * jax.nn.sigmoid(za[:, CONV_CH:])
    zt = lax.dot_general(wt_ref[...], h, (((1,), (1,)), ((), ())),
                         preferred_element_type=F32)
    reps = t // LANES
    gq = jnp.concatenate([gq_ref[...]] * reps, axis=1)
    gk = jnp.concatenate([gk_ref[...]] * reps, axis=1)
    cos_t = cos_ref[...]
    sin_t = sin_ref[...]
    qscale = (HEAD_DIM ** -0.5) * LOG2E
    zero = jnp.zeros((HEAD_DIM, ATT_TQ), BF16)
    for hd in range(ATT_HEADS):
        q = _norm_rope_t(zt[hd * HEAD_DIM:(hd + 1) * HEAD_DIM], gq, cos_t, sin_t)
        q = (q * qscale).astype(BF16)
        qn = jnp.sqrt(_col_norm2(q))
        kv, hh = divmod(hd, HEADS_PER_KV)
        cols = slice(hh * ATT_TQ, (hh + 1) * ATT_TQ)
        for c in range(t // ATT_TQ):
            toks = slice(c * ATT_TQ, (c + 1) * ATT_TQ)
            for j in range(ATT_KV_HEADS):
                qt_ref[0, kv, c, j * HEAD_DIM:(j + 1) * HEAD_DIM, cols] = (
                    q[:, toks] if j == kv else zero)
            qn_ref[0, kv, c, :, cols] = qn[:, toks]
    o = ATT_HEADS * HEAD_DIM
    ks = [_norm_rope_t(zt[o + j * HEAD_DIM:o + (j + 1) * HEAD_DIM], gk, cos_t, sin_t)
          for j in range(ATT_KV_HEADS)]
    for j in range(ATT_KV_HEADS):
        kn2_ref[0, j] = _col_norm2(ks[j].astype(BF16))
    k_ref[0] = jnp.concatenate(ks, axis=0).T.astype(BF16)
    o += ATT_KV_HEADS * HEAD_DIM
    row = lax.broadcasted_iota(jnp.int32, (BF16_ROWS, ATT_TK), 0)
    ones_rows = jnp.where(row == 0, 1.0, 0.0).astype(BF16)
    for j in range(ATT_KV_HEADS):
        v = zt[o + j * HEAD_DIM:o + (j + 1) * HEAD_DIM].astype(BF16)
        for c in range(t // ATT_TK):
            vt_ref[0, j, c, :HEAD_DIM, :] = v[:, c * ATT_TK:(c + 1) * ATT_TK]
            vt_ref[0, j, c, HEAD_DIM:, :] = ones_rows


def _in_proj(x, g, w_a, w_t, gq, gk, cos_t, sin_t):
    b, s, d = x.shape
    t = TOK_TILE
    nt = s // t
    qkv_rows = w_t.shape[0]
    return pl.pallas_call(
        _in_proj_kernel,
        grid=(b, nt),
        in_specs=[
            pl.BlockSpec((1, t, d), lambda i, j: (i, j, 0)),
            _resident((1, d)),
            _resident((d, 2 * CONV_CH)),
            _resident((qkv_rows, d)),
            _resident((HEAD_DIM, LANES)),
            _resident((HEAD_DIM, LANES)),
            pl.BlockSpec((HEAD_DIM, t), lambda i, j: (0, j)),
            pl.BlockSpec((HEAD_DIM, t), lambda i, j: (0, j)),
        ],
        out_specs=[
            pl.BlockSpec((1, t, CONV_CH), lambda i, j: (i, j, 0)),
            pl.BlockSpec((1, ATT_KV_HEADS, t // ATT_TQ, 2 * HEAD_DIM, ATT_N),
                         lambda i, j: (i, 0, j, 0, 0)),
            pl.BlockSpec((1, ATT_KV_HEADS, t // ATT_TQ, 1, ATT_N),
                         lambda i, j: (i, 0, j, 0, 0)),
            pl.BlockSpec((1, t, 2 * HEAD_DIM), lambda i, j: (i, j, 0)),
            pl.BlockSpec((1, ATT_KV_HEADS, 1, t), lambda i, j: (i, 0, 0, j)),
            pl.BlockSpec((1, ATT_KV_HEADS, t // ATT_TK, V_ROWS, ATT_TK),
                         lambda i, j: (i, 0, j, 0, 0)),
        ],
        out_shape=[
            jax.ShapeDtypeStruct((b, s, CONV_CH), F32),
            jax.ShapeDtypeStruct((b, ATT_KV_HEADS, s // ATT_TQ, 2 * HEAD_DIM, ATT_N), BF16),
            jax.ShapeDtypeStruct((b, ATT_KV_HEADS, s // ATT_TQ, 1, ATT_N), F32),
            jax.ShapeDtypeStruct((b, s, 2 * HEAD_DIM), BF16),
            jax.ShapeDtypeStruct((b, ATT_KV_HEADS, 1, s), F32),
            jax.ShapeDtypeStruct((b, ATT_KV_HEADS, s // ATT_TK, V_ROWS, ATT_TK), BF16),
        ],
        compiler_params=pltpu.CompilerParams(
            dimension_semantics=("arbitrary", "arbitrary"),
            vmem_limit_bytes=VMEM_LIMIT),
        name="in_proj",
    )(x, g, w_a, w_t, gq, gk, cos_t, sin_t)


ATT_MIN_DENOM = 2.0 ** -80


def _attn_kernel(qt_ref, qn_ref, k_ref, kn2_ref, vt_ref, o_ref, acc_ref, fin_ref, st_ref,
                 m_ref):
    nq = qt_ref.shape[2]
    nkb = vt_ref.shape[2]

    def k_block(kb):
        return k_ref[0, pl.ds(pl.multiple_of(kb * ATT_TK, ATT_TK), ATT_TK), :]

    kmax = jnp.sqrt(jnp.max(kn2_ref[0, 0], axis=-1, keepdims=True))

    def scores(qb, kb, slot):
        st_ref[slot] = _dot(k_block(kb), qt_ref[0, 0, qb])

    def accumulate(kb, slot, shift):
        p = jnp.exp2(st_ref[slot] - shift).astype(BF16)
        acc_ref[...] += _dot(vt_ref[0, 0, kb], p)

    def exact_block(qb):
        m_ref[...] = jnp.full(m_ref.shape, -jnp.inf, F32)
        acc_ref[...] = jnp.zeros(acc_ref.shape, F32)

        def exact(kb, carry):
            st = _dot(k_block(kb), qt_ref[0, 0, qb])
            m_old = m_ref[...]
            m_new = jnp.maximum(m_old, jnp.max(st, axis=0, keepdims=True))
            p = jnp.exp2(st - m_new).astype(BF16)
            acc_ref[...] = acc_ref[...] * jnp.exp2(m_old - m_new) + _dot(vt_ref[0, 0, kb], p)
            m_ref[...] = m_new
            return carry

        lax.fori_loop(0, nkb, exact, 0)

    scores(0, 0, 0)

    def finish(qb):
        acc = fin_ref[...]
        outs = [acc[:HEAD_DIM, h * ATT_TQ:(h + 1) * ATT_TQ]
                / acc[HEAD_DIM:HEAD_DIM + 1, h * ATT_TQ:(h + 1) * ATT_TQ]
                for h in range(HEADS_PER_KV)]
        rows = pl.ds(pl.multiple_of(qb * ATT_TQ, ATT_TQ), ATT_TQ)
        o_ref[0, rows, :] = jnp.concatenate(outs, axis=0).T.astype(BF16)

    fin_ref[...] = jnp.ones(fin_ref.shape, F32)

    def query_block(qb, carry):
        shift = qn_ref[0, 0, qb] * kmax
        acc_ref[...] = jnp.zeros(acc_ref.shape, F32)

        def group(i, c):
            for u in range(ATT_UNROLL):
                kb = ATT_UNROLL * i + u
                scores(qb, kb + 1, (u + 1) % 2)
                accumulate(kb, u % 2, shift)
            return c

        lax.fori_loop(0, nkb // ATT_UNROLL - 1, group, 0)
        for kb in range(nkb - ATT_UNROLL, nkb):
            if kb + 1 < nkb:
                scores(qb, kb + 1, (kb + 1) % 2)
            else:
                scores(jnp.minimum(qb + 1, nq - 1), 0, 0)
            accumulate(kb, kb % 2, shift)
            if kb == nkb - ATT_UNROLL + ATT_FINISH_AT:
                finish(jnp.maximum(qb - 1, 0))

        denom_ok = jnp.min(acc_ref[HEAD_DIM:HEAD_DIM + 1, :]) >= ATT_MIN_DENOM
        pl.when(jnp.logical_not(denom_ok))(lambda: exact_block(qb))
        fin_ref[...] = acc_ref[...]
        return carry

    lax.fori_loop(0, nq, query_block, 0)
    finish(nq - 1)


def _attention(qt, qn, k, kn2, vt):
    b, s, _ = k.shape
    nq = qt.shape[2]
    nkb = vt.shape[2]
    assert nkb % ATT_UNROLL == 0 and nkb >= 2 * ATT_UNROLL and ATT_UNROLL % 2 == 0
    width = HEADS_PER_KV * HEAD_DIM
    return pl.pallas_call(
        _attn_kernel,
        grid=(b, ATT_KV_HEADS),
        in_specs=[
            pl.BlockSpec((1, 1, nq, 2 * HEAD_DIM, ATT_N), lambda i, j: (i, j, 0, 0, 0)),
            pl.BlockSpec((1, 1, nq, 1, ATT_N), lambda i, j: (i, j, 0, 0, 0)),
            pl.BlockSpec((1, s, 2 * HEAD_DIM), lambda i, j: (i, 0, 0)),
            pl.BlockSpec((1, 1, 1, s), lambda i, j: (i, j, 0, 0)),
            pl.BlockSpec((1, 1, nkb, V_ROWS, ATT_TK), lambda i, j: (i, j, 0, 0, 0)),
        ],
        out_specs=pl.BlockSpec((1, s, width), lambda i, j: (i, 0, j)),
        out_shape=jax.ShapeDtypeStruct((b, s, ATT_KV_HEADS * width), BF16),
        scratch_shapes=[
            pltpu.VMEM((V_ROWS, ATT_N), F32),
            pltpu.VMEM((V_ROWS, ATT_N), F32),
            pltpu.VMEM((2, ATT_TK, ATT_N), F32),
            pltpu.VMEM((1, ATT_N), F32),
        ],
        compiler_params=pltpu.CompilerParams(
            dimension_semantics=("arbitrary", "arbitrary"),
            vmem_limit_bytes=VMEM_LIMIT),
        name="attention",
    )(qt, qn, k, kn2, vt)


def _post_attn_kernel(x_ref, a_ref, ap_ref, an_ref, att_ref, dw_ref, dwb_ref,
                      lng_ref, lnb_ref, wout_ref, gpost_ref, gmem_ref, wq_ref,
                      kt_ref, v_ref, wo_ref, gmpost_ref, o_ref, abuf_ref, ash_ref, cbuf_ref):
    t = x_ref.shape[1]
    j = pl.program_id(1)
    nt = pl.num_programs(1)
    prev = jnp.where(j > 0, ap_ref[0], 0.0)
    nxt = jnp.where(j < nt - 1, an_ref[0], 0.0)
    abuf_ref[:CONV_HALO] = prev
    abuf_ref[CONV_HALO:CONV_HALO + t] = a_ref[0]
    abuf_ref[CONV_HALO + t:] = nxt
    nrows = t + 2 * CONV_HALO
    padded = abuf_ref[...]
    for r in range(1, SUBLANES):
        ash_ref[r - 1] = pltpu.roll(padded, nrows - r, axis=0)
    rows = 64
    base = CONV_HALO - CONV_PAD
    for r0 in range(0, t, rows):
        for c0 in range(0, CONV_CH, LANES):
            lanes = slice(c0, c0 + LANES)
            acc = jnp.broadcast_to(dwb_ref[:, lanes], (rows, LANES))
            for kk in range(CONV_K):
                al, r = divmod(base + kk, SUBLANES)
                lo = al * SUBLANES + r0
                src = abuf_ref if r == 0 else ash_ref.at[r - 1]
                acc = acc + dw_ref[kk:kk + 1, lanes] * src[lo:lo + rows, lanes]
            cbuf_ref[r0:r0 + rows, lanes] = acc
    c = cbuf_ref[...]
    mu = jnp.mean(c, axis=-1, keepdims=True)
    cc = c - mu
    var = jnp.mean(cc * cc, axis=-1, keepdims=True)
    y = cc * lax.rsqrt(var + EPS) * lng_ref[...] + lnb_ref[...]
    y = y * jax.nn.sigmoid(y)
    mixed = jnp.concatenate([y.astype(BF16), att_ref[0]], axis=-1)
    x1 = x_ref[0] + _rms(_dot(mixed, wout_ref[...]), gpost_ref[...])
    h = _rms(x1, gmem_ref[...]).astype(BF16)
    q = _dot(h, wq_ref[...]).astype(BF16)
    heads = []
    for hd in range(MEM_HEADS):
        sl = slice(hd * MEM_HEAD_DIM, (hd + 1) * MEM_HEAD_DIM)
        s = _dot(q[:, sl], kt_ref[0, sl, :])
        p = jnp.exp(s - jnp.max(s, axis=-1, keepdims=True))
        l = jnp.sum(p, axis=-1, keepdims=True)
        heads.append((_dot(p.astype(BF16), v_ref[0, :, sl]) / l).astype(BF16))
    o = jnp.concatenate(heads, axis=-1)
    o_ref[0] = x1 + _rms(_dot(o, wo_ref[...]), gmpost_ref[...])


def _post_attn(x, a, att, dw, dwb, lng, lnb, wout, gpost, gmem, wq, kt, v, wo, gmpost):
    b, s, d = x.shape
    t = TOK_TILE
    nt = s // t
    hb = t // CONV_HALO
    m = v.shape[1]
    vec = _resident((1, d))
    cvec = _resident((1, CONV_CH))
    return pl.pallas_call(
        _post_attn_kernel,
        grid=(b, nt),
        in_specs=[
            pl.BlockSpec((1, t, d), lambda i, j: (i, j, 0)),
            pl.BlockSpec((1, t, CONV_CH), lambda i, j: (i, j, 0)),
            pl.BlockSpec((1, CONV_HALO, CONV_CH),
                         lambda i, j: (i, jnp.maximum(j * hb - 1, 0), 0)),
            pl.BlockSpec((1, CONV_HALO, CONV_CH),
                         lambda i, j: (i, jnp.minimum((j + 1) * hb, nt * hb - 1), 0)),
            pl.BlockSpec((1, t, CONV_CH), lambda i, j: (i, j, 0)),
            _resident((CONV_K, CONV_CH)),
            cvec, cvec, cvec,
            _resident((d, d)),
            vec, vec,
            _resident((d, d)),
            pl.BlockSpec((1, d, m), lambda i, j: (i, 0, 0)),
            pl.BlockSpec((1, m, d), lambda i, j: (i, 0, 0)),
            _resident((d, d)),
            vec,
        ],
        out_specs=pl.BlockSpec((1, t, d), lambda i, j: (i, j, 0)),
        out_shape=jax.ShapeDtypeStruct((b, s, d), F32),
        scratch_shapes=[
            pltpu.VMEM((t + 2 * CONV_HALO, CONV_CH), F32),
            pltpu.VMEM((SUBLANES - 1, t + 2 * CONV_HALO, CONV_CH), F32),
            pltpu.VMEM((t, CONV_CH), F32),
        ],
        compiler_params=pltpu.CompilerParams(
            dimension_semantics=("arbitrary", "arbitrary"),
            vmem_limit_bytes=VMEM_LIMIT),
        name="post_attn",
    )(x, a, a, a, att, dw, dwb, lng, lnb, wout, gpost, gmem, wq, kt, v, wo, gmpost)


def _gelu_tanh(x):
    c = math.sqrt(2.0 / math.pi)
    half = 0.5 * x
    return half + half * jnp.tanh(x * (c + (c * 0.044715) * (x * x)))


def _ffn_kernel(x_ref, xp_ref, xn_ref, g_ref, wup_ref, dw_ref, dwb_ref, wdn_ref,
                gpost_ref, o_ref, hbuf_ref, hb_ref, acc_ref, u_ref):
    t = x_ref.shape[1]
    j = pl.program_id(1)
    nt = pl.num_programs(1)
    g = g_ref[...]
    hbuf_ref[:FFN_HALO] = jnp.where(j > 0, _rms(xp_ref[0], g), 0.0)
    hbuf_ref[FFN_HALO:FFN_HALO + t] = _rms(x_ref[0], g)
    hbuf_ref[FFN_HALO + t:] = jnp.where(j < nt - 1, _rms(xn_ref[0], g), 0.0)
    hb_ref[...] = hbuf_ref[...].astype(BF16)
    acc_ref[...] = jnp.zeros(acc_ref.shape, F32)
    rows = t + 2 * FFN_HALO
    d_ff = wdn_ref.shape[0]
    nc = d_ff // FFN_CHUNK

    def gate_val(ref, c):
        lo = c * FFN_CHUNK
        return jnp.concatenate([ref[:, lo:lo + FFN_CHUNK],
                                ref[:, d_ff + lo:d_ff + lo + FFN_CHUNK]], axis=1)

    def up(c, slot):
        u_ref[slot] = _dot(hb_ref[...], gate_val(wup_ref, c))

    def down(c, slot):
        u = u_ref[slot]
        dw = gate_val(dw_ref, c)
        prev = pltpu.roll(u, 1, axis=0)
        nxt = pltpu.roll(u, rows - 1, axis=0)
        mid = slice(FFN_HALO, FFN_HALO + t)
        conv = (dw[0:1] * prev[mid] + dw[1:2] * u[mid] + dw[2:3] * nxt[mid]
                + gate_val(dwb_ref, c))
        act = _gelu_tanh(conv[:, :FFN_CHUNK]) * conv[:, FFN_CHUNK:]
        acc_ref[...] += _dot(act.astype(BF16),
                             wdn_ref[c * FFN_CHUNK:(c + 1) * FFN_CHUNK, :])

    up(0, 0)
    for c in range(nc):
        if c + 1 < nc:
            up(c + 1, (c + 1) % 2)
        down(c, c % 2)
    o_ref[0] = x_ref[0] + _rms(acc_ref[...], gpost_ref[...])


def _ffn(x, g, wup, dw, dwb, wdn, gpost):
    b, s, d = x.shape
    t = FFN_TILE
    nt = s // t
    hb = t // FFN_HALO
    d_ff = wdn.shape[0]
    return pl.pallas_call(
        _ffn_kernel,
        grid=(b, nt),
        in_specs=[
            pl.BlockSpec((1, t, d), lambda i, j: (i, j, 0)),
            pl.BlockSpec((1, FFN_HALO, d),
                         lambda i, j: (i, jnp.maximum(j * hb - 1, 0), 0)),
            pl.BlockSpec((1, FFN_HALO, d),
                         lambda i, j: (i, jnp.minimum((j + 1) * hb, nt * hb - 1), 0)),
            _resident((1, d)),
            _resident((d, 2 * d_ff)),
            _resident((3, 2 * d_ff)),
            _resident((1, 2 * d_ff)),
            _resident((d_ff, d)),
            _resident((1, d)),
        ],
        out_specs=pl.BlockSpec((1, t, d), lambda i, j: (i, j, 0)),
        out_shape=jax.ShapeDtypeStruct((b, s, d), F32),
        scratch_shapes=[
            pltpu.VMEM((t + 2 * FFN_HALO, d), F32),
            pltpu.VMEM((t + 2 * FFN_HALO, d), BF16),
            pltpu.VMEM((t, d), F32),
            pltpu.VMEM((2, t + 2 * FFN_HALO, 2 * FFN_CHUNK), F32),
        ],
        compiler_params=pltpu.CompilerParams(
            dimension_semantics=("arbitrary", "arbitrary"),
            vmem_limit_bytes=VMEM_LIMIT),
        name="ffn",
    )(x, x, x, g, wup, dw, dwb, wdn, gpost)


def _rope_tables_t(s):
    f32 = np.float32
    rows = s // GRID_W
    r = np.repeat(np.arange(rows, dtype=f32), GRID_W)
    c = np.tile(np.arange(GRID_W, dtype=f32), rows)
    axis_dim = HEAD_DIM // 2
    inv = f32(ROPE_THETA) ** (-np.arange(0, axis_dim, 2, dtype=f32) / f32(axis_dim))
    ang_r = (r[:, None] * inv[None, :]).T.astype(f32)
    ang_c = (c[:, None] * inv[None, :]).T.astype(f32)
    cos_t = np.concatenate([np.cos(ang_r)] * 2 + [np.cos(ang_c)] * 2, axis=0)
    sr, sc = np.sin(ang_r), np.sin(ang_c)
    sin_t = np.concatenate([-sr, sr, -sc, sc], axis=0)
    return jnp.asarray(cos_t, F32), jnp.asarray(sin_t, F32)


def kernel(x, mem, norm_mix_pre, w_in, conv_dw, conv_dw_b, conv_ln_g, conv_ln_b,
           q_norm_g, k_norm_g, w_out, norm_mix_post, norm_mem_pre, mem_norm_g,
           w_mem_q, w_mem_kv, w_mem_o, norm_mem_post, norm_ffn_pre, w_up, ffn_dw,
           ffn_dw_b, w_down, norm_ffn_post):
    depth = w_in.shape[0]
    s = x.shape[1]
    assert s % TOK_TILE == 0 and s % ATT_TK == 0 and TOK_TILE % ATT_TK == 0
    assert w_down.shape[1] % FFN_CHUNK == 0
    cos_t, sin_t = _rope_tables_t(s)
    row = lambda v: v.reshape(1, -1)
    col = lambda v: jnp.broadcast_to(v[:, None], (v.shape[0], LANES))
    for l in range(depth):
        glu_cols = 2 * CONV_CH
        w_a = w_in[l, :, :glu_cols].astype(BF16)
        w_t = w_in[l, :, glu_cols:].T.astype(BF16)
        a, qt, qn, k, kn2, vt = _in_proj(x, row(norm_mix_pre[l]), w_a, w_t,
                                         col(q_norm_g[l]), col(k_norm_g[l]), cos_t, sin_t)
        att = _attention(qt, qn, k, kn2, vt)
        kt_mem, v_mem = _mem_kv(mem, row(mem_norm_g[l]), w_mem_kv[l].astype(BF16))
        x = _post_attn(x, a, att, conv_dw[l], row(conv_dw_b[l]), row(conv_ln_g[l]),
                       row(conv_ln_b[l]), w_out[l].astype(BF16), row(norm_mix_post[l]),
                       row(norm_mem_pre[l]), w_mem_q[l].astype(BF16), kt_mem, v_mem,
                       w_mem_o[l].astype(BF16), row(norm_mem_post[l]))
        x = _ffn(x, row(norm_ffn_pre[l]), w_up[l].astype(BF16), ffn_dw[l],
                 row(ffn_dw_b[l]), w_down[l].astype(BF16), row(norm_ffn_post[l]))
    return x
```

```python
import math

import jax
import jax.numpy as jnp
import numpy as np
from jax import lax
from jax.experimental import pallas as pl
from jax.experimental.pallas import tpu as pltpu

F32 = jnp.float32
BF16 = jnp.bfloat16

GRID_W = 64
HEAD_DIM = 64
ATT_HEADS = 8
ATT_KV_HEADS = 2
HEADS_PER_KV = ATT_HEADS // ATT_KV_HEADS
CONV_CH = 512
CONV_K = 31
CONV_PAD = (CONV_K - 1) // 2
MEM_HEADS = 4
MEM_HEAD_DIM = 256
ROPE_THETA = 10000.0
EPS = 1e-6
LOG2E = 1.4426950408889634

LANES = 128
SUBLANES = 8
BF16_ROWS = 16
VMEM_LIMIT = 56 * 1024 * 1024

TOK_TILE = 512
FFN_TILE = 256
ATT_TQ = 256
ATT_N = HEADS_PER_KV * ATT_TQ
ATT_TK = 256
ATT_UNROLL = 16
ATT_FINISH_AT = 4
V_ROWS =HEAD_DIM + BF16_ROWS
CONV_HALO = 16
FFN_HALO = 8
FFN_CHUNK = 256


def _rms(xf, g):
    return xf * lax.rsqrt(jnp.mean(xf * xf, axis=-1, keepdims=True) + EPS) * g


def _dot(a, b):
    return jnp.dot(a, b, preferred_element_type=F32)


def _resident(shape):
    zeros = (0,) * len(shape)
    return pl.BlockSpec(shape, lambda *_: zeros, pipeline_mode=pl.Buffered(1))


def _mem_kv_kernel(mem_ref, g_ref, w_ref, kt_ref, v_ref):
    d = mem_ref.shape[-1]
    mn = _rms(mem_ref[0], g_ref[...]).astype(BF16)
    kv = _dot(mn, w_ref[...])
    kt_ref[0] = (kv[:, :d] * (MEM_HEAD_DIM ** -0.5)).T.astype(BF16)
    v_ref[0] = kv[:, d:].astype(BF16)


def _mem_kv(mem, g, w_kv):
    b, m, d = mem.shape
    return pl.pallas_call(
        _mem_kv_kernel,
        grid=(b,),
        in_specs=[
            pl.BlockSpec((1, m, d), lambda i: (i, 0, 0)),
            _resident((1, d)),
            _resident((d, 2 * d)),
        ],
        out_specs=[
            pl.BlockSpec((1, d, m), lambda i: (i, 0, 0)),
            pl.BlockSpec((1, m, d), lambda i: (i, 0, 0)),
        ],
        out_shape=[
            jax.ShapeDtypeStruct((b, d, m), BF16),
            jax.ShapeDtypeStruct((b, m, d), BF16),
        ],
        compiler_params=pltpu.CompilerParams(
            dimension_semantics=("arbitrary",), vmem_limit_bytes=VMEM_LIMIT),
        name="mem_kv",
    )(mem, g, w_kv)


def _rope_partner(t):
    q = HEAD_DIM // 4
    return jnp.concatenate([t[q:2 * q], t[:q], t[3 * q:], t[2 * q:3 * q]], axis=0)


def _norm_rope_t(zt, g, cos_t, sin_t):
    r = lax.rsqrt(jnp.mean(zt * zt, axis=0, keepdims=True) + EPS)
    y = zt * r * g
    return y * cos_t + _rope_partner(y) * sin_t


def _col_norm2(t_bf16):
    tf = t_bf16.astype(F32)
    return jnp.sum(tf * tf, axis=0, keepdims=True)


def _in_proj_kernel(x_ref, g_ref, wa_ref, wt_ref, gq_ref, gk_ref, cos_ref, sin_ref,
                    a_ref, qt_ref, qn_ref, k_ref, kn2_ref, vt_ref):
    t = x_ref.shape[1]
    h = _rms(x_ref[0], g_ref[...]).astype(BF16)
    za = _dot(h, wa_ref[...])
    a_ref[0] = za[:, :CONV_CH] * jax.nn.sigmoid(za[:, CONV_CH:])
    zt = lax.dot_general(wt_ref[...], h, (((1,), (1,)), ((), ())),
                         preferred_element_type=F32)
    reps = t // LANES
    gq = jnp.concatenate([gq_ref[...]] * reps, axis=1)
    gk = jnp.concatenate([gk_ref[...]] * reps, axis=1)
    cos_t = cos_ref[...]
    sin_t = sin_ref[...]
    qscale = (HEAD_DIM ** -0.5) * LOG2E
    zero = jnp.zeros((HEAD_DIM, ATT_TQ), BF16)
    for hd in range(ATT_HEADS):
        q = _norm_rope_t(zt[hd * HEAD_DIM:(hd + 1) * HEAD_DIM], gq, cos_t, sin_t)
        q = (q * qscale).astype(BF16)
        qn = jnp.sqrt(_col_norm2(q))
        kv, hh = divmod(hd, HEADS_PER_KV)
        cols = slice(hh * ATT_TQ, (hh + 1) * ATT_TQ)
        for c in range(t // ATT_TQ):
            toks = slice(c * ATT_TQ, (c + 1) * ATT_TQ)
            for j in range(ATT_KV_HEADS):
                qt_ref[0, kv, c, j * HEAD_DIM:(j + 1) * HEAD_DIM, cols] = (
                    q[:, toks] if j == kv else zero)
            qn_ref[0, kv, c, :, cols] = qn[:, toks]
    o = ATT_HEADS * HEAD_DIM
    ks = [_norm_rope_t(zt[o + j * HEAD_DIM:o + (j + 1) * HEAD_DIM], gk, cos_t, sin_t)
          for j in range(ATT_KV_HEADS)]
    for j in range(ATT_KV_HEADS):
        kn2_ref[0, j] = _col_norm2(ks[j].astype(BF16))
    k_ref[0] = jnp.concatenate(ks, axis=0).T.astype(BF16)
    o += ATT_KV_HEADS * HEAD_DIM
    row = lax.broadcasted_iota(jnp.int32, (BF16_ROWS, ATT_TK), 0)
    ones_rows = jnp.where(row == 0, 1.0, 0.0).astype(BF16)
    for j in range(ATT_KV_HEADS):
        v = zt[o + j * HEAD_DIM:o + (j + 1) * HEAD_DIM].astype(BF16)
        for c in range(t // ATT_TK):
            vt_ref[0, j, c, :HEAD_DIM, :] = v[:, c * ATT_TK:(c + 1) * ATT_TK]
            vt_ref[0, j, c, HEAD_DIM:, :] = ones_rows


def _in_proj(x, g, w_a, w_t, gq, gk, cos_t, sin_t):
    b, s, d = x.shape
    t = TOK_TILE
    nt = s // t
    qkv_rows = w_t.shape[0]
    return pl.pallas_call(
        _in_proj_kernel,
        grid=(b, nt),
        in_specs=[
            pl.BlockSpec((1, t, d), lambda i, j: (i, j, 0)),
            _resident((1, d)),
            _resident((d, 2 * CONV_CH)),
            _resident((qkv_rows, d)),
            _resident((HEAD_DIM, LANES)),
            _resident((HEAD_DIM, LANES)),
            pl.BlockSpec((HEAD_DIM, t), lambda i, j: (0, j)),
            pl.BlockSpec((HEAD_DIM, t), lambda i, j: (0, j)),
        ],
        out_specs=[
            pl.BlockSpec((1, t, CONV_CH), lambda i, j: (i, j, 0)),
            pl.BlockSpec((1, ATT_KV_HEADS, t // ATT_TQ, 2 * HEAD_DIM, ATT_N),
                         lambda i, j: (i, 0, j, 0, 0)),
            pl.BlockSpec((1, ATT_KV_HEADS, t // ATT_TQ, 1, ATT_N),
                         lambda i, j: (i, 0, j, 0, 0)),
            pl.BlockSpec((1, t, 2 * HEAD_DIM), lambda i, j: (i, j, 0)),
            pl.BlockSpec((1, ATT_KV_HEADS, 1, t), lambda i, j: (i, 0, 0, j)),
            pl.BlockSpec((1, ATT_KV_HEADS, t // ATT_TK, V_ROWS, ATT_TK),
                         lambda i, j: (i, 0, j, 0, 0)),
        ],
        out_shape=[
            jax.ShapeDtypeStruct((b, s, CONV_CH), F32),
            jax.ShapeDtypeStruct((b, ATT_KV_HEADS, s // ATT_TQ, 2 * HEAD_DIM, ATT_N), BF16),
            jax.ShapeDtypeStruct((b, ATT_KV_HEADS, s // ATT_TQ, 1, ATT_N), F32),
            jax.ShapeDtypeStruct((b, s, 2 * HEAD_DIM), BF16),
            jax.ShapeDtypeStruct((b, ATT_KV_HEADS, 1, s), F32),
            jax.ShapeDtypeStruct((b, ATT_KV_HEADS, s // ATT_TK, V_ROWS, ATT_TK), BF16),
        ],
        compiler_params=pltpu.CompilerParams(
            dimension_semantics=("arbitrary", "arbitrary"),
            vmem_limit_bytes=VMEM_LIMIT),
        name="in_proj",
    )(x, g, w_a, w_t, gq, gk, cos_t, sin_t)


ATT_MIN_DENOM = 2.0 ** -80


def _attn_kernel(qt_ref, qn_ref, k_ref, kn2_ref, vt_ref, o_ref, acc_ref, fin_ref, st_ref,
                 m_ref):
    nq = qt_ref.shape[2]
    nkb = vt_ref.shape[2]

    def k_block(kb):
        return k_ref[0, pl.ds(pl.multiple_of(kb * ATT_TK, ATT_TK), ATT_TK), :]

    kmax = jnp.sqrt(jnp.max(kn2_ref[0, 0], axis=-1, keepdims=True))

    def scores(qb, kb, slot):
        st_ref[slot] = _dot(k_block(kb), qt_ref[0, 0, qb])

    def accumulate(kb, slot, shift):
        p = jnp.exp2(st_ref[slot] - shift).astype(BF16)
        acc_ref[...] += _dot(vt_ref[0, 0, kb], p)

    def exact_block(qb):
        m_ref[...] = jnp.full(m_ref.shape, -jnp.inf, F32)
        acc_ref[...] = jnp.zeros(acc_ref.shape, F32)

        def exact(kb, carry):
            st = _dot(k_block(kb), qt_ref[0, 0, qb])
            m_old = m_ref[...]
            m_new = jnp.maximum(m_old, jnp.max(st, axis=0, keepdims=True))
            p = jnp.exp2(st - m_new).astype(BF16)
            acc_ref[...] = acc_ref[...] * jnp.exp2(m_old - m_new) + _dot(vt_ref[0, 0, kb], p)
            m_ref[...] = m_new
            return carry

        lax.fori_loop(0, nkb, exact, 0)

    scores(0, 0, 0)

    def finish(qb):
        acc = fin_ref[...]
        outs = [acc[:HEAD_DIM, h * ATT_TQ:(h + 1) * ATT_TQ]
                / acc[HEAD_DIM:HEAD_DIM + 1, h * ATT_TQ:(h + 1) * ATT_TQ]
                for h in range(HEADS_PER_KV)]
        rows = pl.ds(pl.multiple_of(qb * ATT_TQ, ATT_TQ), ATT_TQ)
        o_ref[0, rows, :] = jnp.concatenate(outs, axis=0).T.astype(BF16)

    fin_ref[...] = jnp.ones(fin_ref.shape, F32)

    def query_block(qb, carry):
        shift = qn_ref[0, 0, qb] * kmax
        acc_ref[...] = jnp.zeros(acc_ref.shape, F32)

        def group(i, c):
            for u in range(ATT_UNROLL):
                kb = ATT_UNROLL * i + u
                scores(qb, kb + 1, (u + 1) % 2)
                accumulate(kb, u % 2, shift)
            return c

        lax.fori_loop(0, nkb // ATT_UNROLL - 1, group, 0)
        for kb in range(nkb - ATT_UNROLL, nkb):
            if kb + 1 < nkb:
                scores(qb, kb + 1, (kb + 1) % 2)
            else:
                scores(jnp.minimum(qb + 1, nq - 1), 0, 0)
            accumulate(kb, kb % 2, shift)
            if kb == nkb - ATT_UNROLL + ATT_FINISH_AT:
                finish(jnp.maximum(qb - 1, 0))

        denom_ok = jnp.min(acc_ref[HEAD_DIM:HEAD_DIM + 1, :]) >= ATT_MIN_DENOM
        pl.when(jnp.logical_not(denom_ok))(lambda: exact_block(qb))
        fin_ref[...] = acc_ref[...]
        return carry

    lax.fori_loop(0, nq, query_block, 0)
    finish(nq - 1)


def _attention(qt, qn, k, kn2, vt):
    b, s, _ = k.shape
    nq = qt.shape[2]
    nkb = vt.shape[2]
    assert nkb % ATT_UNROLL == 0 and nkb >= 2 * ATT_UNROLL and ATT_UNROLL % 2 == 0
    width = HEADS_PER_KV * HEAD_DIM
    return pl.pallas_call(
        _attn_kernel,
        grid=(b, ATT_KV_HEADS),
        in_specs=[
            pl.BlockSpec((1, 1, nq, 2 * HEAD_DIM, ATT_N), lambda i, j: (i, j, 0, 0, 0)),
            pl.BlockSpec((1, 1, nq, 1, ATT_N), lambda i, j: (i, j, 0, 0, 0)),
            pl.BlockSpec((1, s, 2 * HEAD_DIM), lambda i, j: (i, 0, 0)),
            pl.BlockSpec((1, 1, 1, s), lambda i, j: (i, j, 0, 0)),
            pl.BlockSpec((1, 1, nkb, V_ROWS, ATT_TK), lambda i, j: (i, j, 0, 0, 0)),
        ],
        out_specs=pl.BlockSpec((1, s, width), lambda i, j: (i, 0, j)),
        out_shape=jax.ShapeDtypeStruct((b, s, ATT_KV_HEADS * width), BF16),
        scratch_shapes=[
            pltpu.VMEM((V_ROWS, ATT_N), F32),
            pltpu.VMEM((V_ROWS, ATT_N), F32),
            pltpu.VMEM((2, ATT_TK, ATT_N), F32),
            pltpu.VMEM((1, ATT_N), F32),
        ],
        compiler_params=pltpu.CompilerParams(
            dimension_semantics=("arbitrary", "arbitrary"),
            vmem_limit_bytes=VMEM_LIMIT),
        name="attention",
    )(qt, qn, k, kn2, vt)


def _post_attn_kernel(x_ref, a_ref, ap_ref, an_ref, att_ref, dw_ref, dwb_ref,
                      lng_ref, lnb_ref, wout_ref, gpost_ref, gmem_ref, wq_ref,
                      kt_ref, v_ref, wo_ref, gmpost_ref, o_ref, abuf_ref, ash_ref, cbuf_ref):
    t = x_ref.shape[1]
    j = pl.program_id(1)
    nt = pl.num_programs(1)
    prev = jnp.where(j > 0, ap_ref[0], 0.0)
    nxt = jnp.where(j < nt - 1, an_ref[0], 0.0)
    abuf_ref[:CONV_HALO] = prev
    abuf_ref[CONV_HALO:CONV_HALO + t] = a_ref[0]
    abuf_ref[CONV_HALO + t:] = nxt
    nrows = t + 2 * CONV_HALO
    padded = abuf_ref[...]
    for r in range(1, SUBLANES):
        ash_ref[r - 1] = pltpu.roll(padded, nrows - r, axis=0)
    rows = 64
    base = CONV_HALO - CONV_PAD
    for r0 in range(0, t, rows):
        for c0 in range(0, CONV_CH, LANES):
            lanes = slice(c0, c0 + LANES)
            acc = jnp.broadcast_to(dwb_ref[:, lanes], (rows, LANES))
            for kk in range(CONV_K):
                al, r = divmod(base + kk, SUBLANES)
                lo = al * SUBLANES + r0
                src = abuf_ref if r == 0 else ash_ref.at[r - 1]
                acc = acc + dw_ref[kk:kk + 1, lanes] * src[lo:lo + rows, lanes]
            cbuf_ref[r0:r0 + rows, lanes] = acc
    c = cbuf_ref[...]
    mu = jnp.mean(c, axis=-1, keepdims=True)
    cc = c - mu
    var = jnp.mean(cc * cc, axis=-1, keepdims=True)
    y = cc * lax.rsqrt(var + EPS) * lng_ref[...] + lnb_ref[...]
    y = y * jax.nn.sigmoid(y)
    mixed = jnp.concatenate([y.astype(BF16), att_ref[0]], axis=-1)
    x1 = x_ref[0] + _rms(_dot(mixed, wout_ref[...]), gpost_ref[...])
    h = _rms(x1, gmem_ref[...]).astype(BF16)
    q = _dot(h, wq_ref[...]).astype(BF16)
    heads = []
    for hd in range(MEM_HEADS):
        sl = slice(hd * MEM_HEAD_DIM, (hd + 1) * MEM_HEAD_DIM)
        s = _dot(q[:, sl], kt_ref[0, sl, :])
        p = jnp.exp(s - jnp.max(s, axis=-1, keepdims=True))
        l = jnp.sum(p, axis=-1, keepdims=True)
        heads.append((_dot(p.astype(BF16), v_ref[0, :, sl]) / l).astype(BF16))
    o = jnp.concatenate(heads, axis=-1)
    o_ref[0] = x1 + _rms(_dot(o, wo_ref[...]), gmpost_ref[...])


def _post_attn(x, a, att, dw, dwb, lng, lnb, wout, gpost, gmem, wq, kt, v, wo, gmpost):
    b, s, d = x.shape
    t = TOK_TILE
    nt = s // t
    hb = t // CONV_HALO
    m = v.shape[1]
    vec = _resident((1, d))
    cvec = _resident((1, CONV_CH))
    return pl.pallas_call(
        _post_attn_kernel,
        grid=(b, nt),
        in_specs=[
            pl.BlockSpec((1, t, d), lambda i, j: (i, j, 0)),
            pl.BlockSpec((1, t, CONV_CH), lambda i, j: (i, j, 0)),
            pl.BlockSpec((1, CONV_HALO, CONV_CH),
                         lambda i, j: (i, jnp.maximum(j * hb - 1, 0), 0)),
            pl.BlockSpec((1, CONV_HALO, CONV_CH),
                         lambda i, j: (i, jnp.minimum((j + 1) * hb, nt * hb - 1), 0)),
            pl.BlockSpec((1, t, CONV_CH), lambda i, j: (i, j, 0)),
            _resident((CONV_K, CONV_CH)),
            cvec, cvec, cvec,
            _resident((d, d)),
            vec, vec,
            _resident((d, d)),
            pl.BlockSpec((1, d, m), lambda i, j: (i, 0, 0)),
            pl.BlockSpec((1, m, d), lambda i, j: (i, 0, 0)),
            _resident((d, d)),
            vec,
        ],
        out_specs=pl.BlockSpec((1, t, d), lambda i, j: (i, j, 0)),
        out_shape=jax.ShapeDtypeStruct((b, s, d), F32),
        scratch_shapes=[
            pltpu.VMEM((t + 2 * CONV_HALO, CONV_CH), F32),
            pltpu.VMEM((SUBLANES - 1, t + 2 * CONV_HALO, CONV_CH), F32),
            pltpu.VMEM((t, CONV_CH), F32),
        ],
        compiler_params=pltpu.CompilerParams(
            dimension_semantics=("arbitrary", "arbitrary"),
            vmem_limit_bytes=VMEM_LIMIT),
        name="post_attn",
    )(x, a, a, a, att, dw, dwb, lng, lnb, wout, gpost, gmem, wq, kt, v, wo, gmpost)


def _gelu_tanh(x):
    c = math.sqrt(2.0 / math.pi)
    half = 0.5 * x
    return half + half * jnp.tanh(x * (c + (c * 0.044715) * (x * x)))


def _ffn_kernel(x_ref, xp_ref, xn_ref, g_ref, wup_ref, dw_ref, dwb_ref, wdn_ref,
                gpost_ref, o_ref, hbuf_ref, hb_ref, act_ref, u_ref):
    t = x_ref.shape[1]
    j = pl.program_id(1)
    nt = pl.num_programs(1)
    g = g_ref[...]
    hbuf_ref[:FFN_HALO] = jnp.where(j > 0, _rms(xp_ref[0], g), 0.0)
    hbuf_ref[FFN_HALO:FFN_HALO + t] = _rms(x_ref[0], g)
    hbuf_ref[FFN_HALO + t:] = jnp.where(j < nt - 1, _rms(xn_ref[0], g), 0.0)
    hb_ref[...] = hbuf_ref[...].astype(BF16)
    rows = t + 2 * FFN_HALO
    d_ff = wdn_ref.shape[0]
    nc = d_ff // FFN_CHUNK

    def gate_val(ref, c):
        lo = c * FFN_CHUNK
        return jnp.concatenate([ref[:, lo:lo + FFN_CHUNK],
                                ref[:, d_ff + lo:d_ff + lo + FFN_CHUNK]], axis=1)

    def up(c, slot):
        u_ref[slot] = _dot(hb_ref[...], gate_val(wup_ref, c))

    def down(c, slot):
        u = u_ref[slot]
        dw = gate_val(dw_ref, c)
        prev = pltpu.roll(u, 1, axis=0)
        nxt = pltpu.roll(u, rows - 1, axis=0)
        mid = slice(FFN_HALO, FFN_HALO + t)
        conv = (dw[0:1] * prev[mid] + dw[1:2] * u[mid] + dw[2:3] * nxt[mid]
                + gate_val(dwb_ref, c))
        act = _gelu_tanh(conv[:, :FFN_CHUNK]) * conv[:, FFN_CHUNK:]
        act_ref[:, c * FFN_CHUNK:(c + 1) * FFN_CHUNK] = act.astype(BF16)

    up(0, 0)
    for c in range(nc):
        if c + 1 < nc:
            up(c + 1, (c + 1) % 2)
        down(c, c % 2)
    o_ref[0] = x_ref[0] + _rms(_dot(act_ref[...], wdn_ref[...]), gpost_ref[...])


def _ffn(x, g, wup, dw, dwb, wdn, gpost):
    b, s, d = x.shape
    t = FFN_TILE
    nt = s // t
    hb = t // FFN_HALO
    d_ff = wdn.shape[0]
    return pl.pallas_call(
        _ffn_kernel,
        grid=(b, nt),
        in_specs=[
            pl.BlockSpec((1, t, d), lambda i, j: (i, j, 0)),
            pl.BlockSpec((1, FFN_HALO, d),
                         lambda i, j: (i, jnp.maximum(j * hb - 1, 0), 0)),
            pl.BlockSpec((1, FFN_HALO, d),
                         lambda i, j: (i, jnp.minimum((j + 1) * hb, nt * hb - 1), 0)),
            _resident((1, d)),
            _resident((d, 2 * d_ff)),
            _resident((3, 2 * d_ff)),
            _resident((1, 2 * d_ff)),
            _resident((d_ff, d)),
            _resident((1, d)),
        ],
        out_specs=pl.BlockSpec((1, t, d), lambda i, j: (i, j, 0)),
        out_shape=jax.ShapeDtypeStruct((b, s, d), F32),
        scratch_shapes=[
            pltpu.VMEM((t + 2 * FFN_HALO, d), F32),
            pltpu.VMEM((t + 2 * FFN_HALO, d), BF16),
            pltpu.VMEM((t, d_ff), BF16),
            pltpu.VMEM((2, t + 2 * FFN_HALO, 2 * FFN_CHUNK), F32),
        ],
        compiler_params=pltpu.CompilerParams(
            dimension_semantics=("arbitrary", "arbitrary"),
            vmem_limit_bytes=VMEM_LIMIT),
        name="ffn",
    )(x, x, x, g, wup, dw, dwb, wdn, gpost)


def _rope_tables_t(s):
    f32 = np.float32
    rows = s // GRID_W
    r = np.repeat(np.arange(rows, dtype=f32), GRID_W)
    c = np.tile(np.arange(GRID_W, dtype=f32), rows)
    axis_dim = HEAD_DIM // 2
    inv = f32(ROPE_THETA) ** (-np.arange(0, axis_dim, 2, dtype=f32) / f32(axis_dim))
    ang_r = (r[:, None] * inv[None, :]).T.astype(f32)
    ang_c = (c[:, None] * inv[None, :]).T.astype(f32)
    cos_t = np.concatenate([np.cos(ang_r)] * 2 + [np.cos(ang_c)] * 2, axis=0)
    sr, sc = np.sin(ang_r), np.sin(ang_c)
    sin_t = np.concatenate([-sr, sr, -sc, sc], axis=0)
    return jnp.asarray(cos_t, F32), jnp.asarray(sin_t, F32)


def kernel(x, mem, norm_mix_pre, w_in, conv_dw, conv_dw_b, conv_ln_g, conv_ln_b,
           q_norm_g, k_norm_g, w_out, norm_mix_post, norm_mem_pre, mem_norm_g,
           w_mem_q, w_mem_kv, w_mem_o, norm_mem_post, norm_ffn_pre, w_up, ffn_dw,
           ffn_dw_b, w_down, norm_ffn_post):
    depth = w_in.shape[0]
    s = x.shape[1]
    assert s % TOK_TILE == 0 and s % ATT_TK == 0 and TOK_TILE % ATT_TK == 0
    assert w_down.shape[1] % FFN_CHUNK == 0
    cos_t, sin_t = _rope_tables_t(s)
    row = lambda v: v.reshape(1, -1)
    col = lambda v: jnp.broadcast_to(v[:, None], (v.shape[0], LANES))
    for l in range(depth):
        glu_cols = 2 * CONV_CH
        w_a = w_in[l, :, :glu_cols].astype(BF16)
        w_t = w_in[l, :, glu_cols:].T.astype(BF16)
        a, qt, qn, k, kn2, vt = _in_proj(x, row(norm_mix_pre[l]), w_a, w_t,
                                         col(q_norm_g[l]), col(k_norm_g[l]), cos_t, sin_t)
        att = _attention(qt, qn, k, kn2, vt)
        kt_mem, v_mem = _mem_kv(mem, row(mem_norm_g[l]), w_mem_kv[l].astype(BF16))
        x = _post_attn(x, a, att, conv_dw[l], row(conv_dw_b[l]), row(conv_ln_g[l]),
                       row(conv_ln_b[l]), w_out[l].astype(BF16), row(norm_mix_post[l]),
                       row(norm_mem_pre[l]), w_mem_q[l].astype(BF16), kt_mem, v_mem,
                       w_mem_o[l].astype(BF16), row(norm_mem_post[l]))
        x = _ffn(x, row(norm_ffn_pre[l]), w_up[l].astype(BF16), ffn_dw[l],
                 row(ffn_dw_b[l]), w_down[l].astype(BF16), row(norm_ffn_post[l]))
    return x
```

```python
import math

import jax
import jax.numpy as jnp
import numpy as np
from jax import lax
from jax.experimental import pallas as pl
from jax.experimental.pallas import tpu as pltpu

F32 = jnp.float32
BF16 = jnp.bfloat16

GRID_W = 64
HEAD_DIM = 64
ATT_HEADS = 8
ATT_KV_HEADS = 2
HEADS_PER_KV = ATT_HEADS // ATT_KV_HEADS
CONV_CH = 512
CONV_K = 31
CONV_PAD = (CONV_K - 1) // 2
MEM_HEADS = 4
MEM_HEAD_DIM = 256
ROPE_THETA = 10000.0
EPS = 1e-6
LOG2E = 1.4426950408889634

LANES = 128
SUBLANES = 8
BF16_ROWS = 16
VMEM_LIMIT = 56 * 1024 * 1024

TOK_TILE = 1024
FFN_TILE = 256
ATT_TQ = 256
ATT_N = HEADS_PER_KV * ATT_TQ
ATT_TK = 256
ATT_UNROLL = 16
ATT_FINISH_AT = 4
V_ROWS =HEAD_DIM + BF16_ROWS
CONV_HALO = 16
FFN_HALO = 8
FFN_CHUNK = 256


def _rms(xf, g):
    return xf * lax.rsqrt(jnp.mean(xf * xf, axis=-1, keepdims=True) + EPS) * g


def _dot(a, b):
    return jnp.dot(a, b, preferred_element_type=F32)


def _resident(shape):
    zeros = (0,) * len(shape)
    return pl.BlockSpec(shape, lambda *_: zeros, pipeline_mode=pl.Buffered(1))


def _mem_kv_kernel(mem_ref, g_ref, w_ref, kt_ref, v_ref):
    d = mem_ref.shape[-1]
    mn = _rms(mem_ref[0], g_ref[...]).astype(BF16)
    kv = _dot(mn, w_ref[...])
    kt_ref[0] = (kv[:, :d] * (MEM_HEAD_DIM ** -0.5)).T.astype(BF16)
    v_ref[0] = kv[:, d:].astype(BF16)


def _mem_kv(mem, g, w_kv):
    b, m, d = mem.shape
    return pl.pallas_call(
        _mem_kv_kernel,
        grid=(b,),
        in_specs=[
            pl.BlockSpec((1, m, d), lambda i: (i, 0, 0)),
            _resident((1, d)),
            _resident((d, 2 * d)),
        ],
        out_specs=[
            pl.BlockSpec((1, d, m), lambda i: (i, 0, 0)),
            pl.BlockSpec((1, m, d), lambda i: (i, 0, 0)),
        ],
        out_shape=[
            jax.ShapeDtypeStruct((b, d, m), BF16),
            jax.ShapeDtypeStruct((b, m, d), BF16),
        ],
        compiler_params=pltpu.CompilerParams(
            dimension_semantics=("arbitrary",), vmem_limit_bytes=VMEM_LIMIT),
        name="mem_kv",
    )(mem, g, w_kv)


def _rope_partner(t):
    q = HEAD_DIM // 4
    return jnp.concatenate([t[q:2 * q], t[:q], t[3 * q:], t[2 * q:3 * q]], axis=0)


def _norm_rope_t(zt, g, cos_t, sin_t):
    r = lax.rsqrt(jnp.mean(zt * zt, axis=0, keepdims=True) + EPS)
    y = zt * r * g
    return y * cos_t + _rope_partner(y) * sin_t


def _col_norm2(t_bf16):
    tf = t_bf16.astype(F32)
    return jnp.sum(tf * tf, axis=0, keepdims=True)


def _in_proj_kernel(x_ref, g_ref, wa_ref, wt_ref, gq_ref, gk_ref, cos_ref, sin_ref,
                    a_ref, qt_ref, qn_ref, k_ref, kn2_ref, vt_ref):
    t = x_ref.shape[1]
    h = _rms(x_ref[0], g_ref[...]).astype(BF16)
    za = _dot(h, wa_ref[...])
    a_ref[0] = za[:, :CONV_CH] * jax.nn.sigmoid(za[:, CONV_CH:])
    zt = lax.dot_general(wt_ref[...], h, (((1,), (1,)), ((), ())),
                         preferred_element_type=F32)
    reps = t // LANES
    gq = jnp.concatenate([gq_ref[...]] * reps, axis=1)
    gk = jnp.concatenate([gk_ref[...]] * reps, axis=1)
    cos_t = cos_ref[...]
    sin_t = sin_ref[...]
    qscale = (HEAD_DIM ** -0.5) * LOG2E
    zero = jnp.zeros((HEAD_DIM, ATT_TQ), BF16)
    for hd in range(ATT_HEADS):
        q = _norm_rope_t(zt[hd * HEAD_DIM:(hd + 1) * HEAD_DIM], gq, cos_t, sin_t)
        q = (q * qscale).astype(BF16)
        qn = jnp.sqrt(_col_norm2(q))
        kv, hh = divmod(hd, HEADS_PER_KV)
        cols = slice(hh * ATT_TQ, (hh + 1) * ATT_TQ)
        for c in range(t // ATT_TQ):
            toks = slice(c * ATT_TQ, (c + 1) * ATT_TQ)
            for j in range(ATT_KV_HEADS):
                qt_ref[0, kv, c, j * HEAD_DIM:(j + 1) * HEAD_DIM, cols] = (
                    q[:, toks] if j == kv else zero)
            qn_ref[0, kv, c, :, cols] = qn[:, toks]
    o = ATT_HEADS * HEAD_DIM
    ks = [_norm_rope_t(zt[o + j * HEAD_DIM:o + (j + 1) * HEAD_DIM], gk, cos_t, sin_t)
          for j in range(ATT_KV_HEADS)]
    for j in range(ATT_KV_HEADS):
        kn2_ref[0, j] = _col_norm2(ks[j].astype(BF16))
    k_ref[0] = jnp.concatenate(ks, axis=0).T.astype(BF16)
    o += ATT_KV_HEADS * HEAD_DIM
    row = lax.broadcasted_iota(jnp.int32, (BF16_ROWS, ATT_TK), 0)
    ones_rows = jnp.where(row == 0, 1.0, 0.0).astype(BF16)
    for j in range(ATT_KV_HEADS):
        v = zt[o + j * HEAD_DIM:o + (j + 1) * HEAD_DIM].astype(BF16)
        for c in range(t // ATT_TK):
            vt_ref[0, j, c, :HEAD_DIM, :] = v[:, c * ATT_TK:(c + 1) * ATT_TK]
            vt_ref[0, j, c, HEAD_DIM:, :] = ones_rows


def _in_proj(x, g, w_a, w_t, gq, gk, cos_t, sin_t):
    b, s, d = x.shape
    t = TOK_TILE
    nt = s // t
    qkv_rows = w_t.shape[0]
    return pl.pallas_call(
        _in_proj_kernel,
        grid=(b, nt),
        in_specs=[
            pl.BlockSpec((1, t, d), lambda i, j: (i, j, 0)),
            _resident((1, d)),
            _resident((d, 2 * CONV_CH)),
            _resident((qkv_rows, d)),
            _resident((HEAD_DIM, LANES)),
            _resident((HEAD_DIM, LANES)),
            pl.BlockSpec((HEAD_DIM, t), lambda i, j: (0, j)),
            pl.BlockSpec((HEAD_DIM, t), lambda i, j: (0, j)),
        ],
        out_specs=[
            pl.BlockSpec((1, t, CONV_CH), lambda i, j: (i, j, 0)),
            pl.BlockSpec((1, ATT_KV_HEADS, t // ATT_TQ, 2 * HEAD_DIM, ATT_N),
                         lambda i, j: (i, 0, j, 0, 0)),
            pl.BlockSpec((1, ATT_KV_HEADS, t // ATT_TQ, 1, ATT_N),
                         lambda i, j: (i, 0, j, 0, 0)),
            pl.BlockSpec((1, t, 2 * HEAD_DIM), lambda i, j: (i, j, 0)),
            pl.BlockSpec((1, ATT_KV_HEADS, 1, t), lambda i, j: (i, 0, 0, j)),
            pl.BlockSpec((1, ATT_KV_HEADS, t // ATT_TK, V_ROWS, ATT_TK),
                         lambda i, j: (i, 0, j, 0, 0)),
        ],
        out_shape=[
            jax.ShapeDtypeStruct((b, s, CONV_CH), F32),
            jax.ShapeDtypeStruct((b, ATT_KV_HEADS, s // ATT_TQ, 2 * HEAD_DIM, ATT_N), BF16),
            jax.ShapeDtypeStruct((b, ATT_KV_HEADS, s // ATT_TQ, 1, ATT_N), F32),
            jax.ShapeDtypeStruct((b, s, 2 * HEAD_DIM), BF16),
            jax.ShapeDtypeStruct((b, ATT_KV_HEADS, 1, s), F32),
            jax.ShapeDtypeStruct((b, ATT_KV_HEADS, s // ATT_TK, V_ROWS, ATT_TK), BF16),
        ],
        compiler_params=pltpu.CompilerParams(
            dimension_semantics=("arbitrary", "arbitrary"),
            vmem_limit_bytes=VMEM_LIMIT),
        name="in_proj",
    )(x, g, w_a, w_t, gq, gk, cos_t, sin_t)


ATT_MIN_DENOM = 2.0 ** -80


def _attn_kernel(qt_ref, qn_ref, k_ref, kn2_ref, vt_ref, o_ref, acc_ref, fin_ref, st_ref,
                 m_ref):
    nq = qt_ref.shape[2]
    nkb = vt_ref.shape[2]

    def k_block(kb):
        return k_ref[0, pl.ds(pl.multiple_of(kb * ATT_TK, ATT_TK), ATT_TK), :]

    kmax = jnp.sqrt(jnp.max(kn2_ref[0, 0], axis=-1, keepdims=True))

    def scores(qb, kb, slot):
        st_ref[slot] = _dot(k_block(kb), qt_ref[0, 0, qb])

    def accumulate(kb, slot, shift):
        p = jnp.exp2(st_ref[slot] - shift).astype(BF16)
        acc_ref[...] += _dot(vt_ref[0, 0, kb], p)

    def exact_block(qb):
        m_ref[...] = jnp.full(m_ref.shape, -jnp.inf, F32)
        acc_ref[...] = jnp.zeros(acc_ref.shape, F32)

        def exact(kb, carry):
            st = _dot(k_block(kb), qt_ref[0, 0, qb])
            m_old = m_ref[...]
            m_new = jnp.maximum(m_old, jnp.max(st, axis=0, keepdims=True))
            p = jnp.exp2(st - m_new).astype(BF16)
            acc_ref[...] = acc_ref[...] * jnp.exp2(m_old - m_new) + _dot(vt_ref[0, 0, kb], p)
            m_ref[...] = m_new
            return carry

        lax.fori_loop(0, nkb, exact, 0)

    scores(0, 0, 0)

    def finish(qb):
        acc = fin_ref[...]
        outs = [acc[:HEAD_DIM, h * ATT_TQ:(h + 1) * ATT_TQ]
                / acc[HEAD_DIM:HEAD_DIM + 1, h * ATT_TQ:(h + 1) * ATT_TQ]
                for h in range(HEADS_PER_KV)]
        rows = pl.ds(pl.multiple_of(qb * ATT_TQ, ATT_TQ), ATT_TQ)
        o_ref[0, rows, :] = jnp.concatenate(outs, axis=0).T.astype(BF16)

    fin_ref[...] = jnp.ones(fin_ref.shape, F32)

    def query_block(qb, carry):
        shift = qn_ref[0, 0, qb] * kmax
        acc_ref[...] = jnp.zeros(acc_ref.shape, F32)

        def group(i, c):
            for u in range(ATT_UNROLL):
                kb = ATT_UNROLL * i + u
                scores(qb, kb + 1, (u + 1) % 2)
                accumulate(kb, u % 2, shift)
            return c

        lax.fori_loop(0, nkb // ATT_UNROLL - 1, group, 0)
        for kb in range(nkb - ATT_UNROLL, nkb):
            if kb + 1 < nkb:
                scores(qb, kb + 1, (kb + 1) % 2)
            else:
                scores(jnp.minimum(qb + 1, nq - 1), 0, 0)
            accumulate(kb, kb % 2, shift)
            if kb == nkb - ATT_UNROLL + ATT_FINISH_AT:
                finish(jnp.maximum(qb - 1, 0))

        denom_ok = jnp.min(acc_ref[HEAD_DIM:HEAD_DIM + 1, :]) >= ATT_MIN_DENOM
        pl.when(jnp.logical_not(denom_ok))(lambda: exact_block(qb))
        fin_ref[...] = acc_ref[...]
        return carry

    lax.fori_loop(0, nq, query_block, 0)
    finish(nq - 1)


def _attention(qt, qn, k, kn2, vt):
    b, s, _ = k.shape
    nq = qt.shape[2]
    nkb = vt.shape[2]
    assert nkb % ATT_UNROLL == 0 and nkb >= 2 * ATT_UNROLL and ATT_UNROLL % 2 == 0
    width = HEADS_PER_KV * HEAD_DIM
    return pl.pallas_call(
        _attn_kernel,
        grid=(b, ATT_KV_HEADS),
        in_specs=[
            pl.BlockSpec((1, 1, nq, 2 * HEAD_DIM, ATT_N), lambda i, j: (i, j, 0, 0, 0)),
            pl.BlockSpec((1, 1, nq, 1, ATT_N), lambda i, j: (i, j, 0, 0, 0)),
            pl.BlockSpec((1, s, 2 * HEAD_DIM), lambda i, j: (i, 0, 0)),
            pl.BlockSpec((1, 1, 1, s), lambda i, j: (i, j, 0, 0)),
            pl.BlockSpec((1, 1, nkb, V_ROWS, ATT_TK), lambda i, j: (i, j, 0, 0, 0)),
        ],
        out_specs=pl.BlockSpec((1, s, width), lambda i, j: (i, 0, j)),
        out_shape=jax.ShapeDtypeStruct((b, s, ATT_KV_HEADS * width), BF16),
        scratch_shapes=[
            pltpu.VMEM((V_ROWS, ATT_N), F32),
            pltpu.VMEM((V_ROWS, ATT_N), F32),
            pltpu.VMEM((2, ATT_TK, ATT_N), F32),
            pltpu.VMEM((1, ATT_N), F32),
        ],
        compiler_params=pltpu.CompilerParams(
            dimension_semantics=("arbitrary", "arbitrary"),
            vmem_limit_bytes=VMEM_LIMIT),
        name="attention",
    )(qt, qn, k, kn2, vt)


def _post_attn_kernel(x_ref, a_ref, ap_ref, an_ref, att_ref, dw_ref, dwb_ref,
                      lng_ref, lnb_ref, wout_ref, gpost_ref, gmem_ref, wq_ref,
                      kt_ref, v_ref, wo_ref, gmpost_ref, o_ref, abuf_ref, ash_ref, cbuf_ref):
    t = x_ref.shape[1]
    j = pl.program_id(1)
    nt = pl.num_programs(1)
    prev = jnp.where(j > 0, ap_ref[0], 0.0)
    nxt = jnp.where(j < nt - 1, an_ref[0], 0.0)
    abuf_ref[:CONV_HALO] = prev
    abuf_ref[CONV_HALO:CONV_HALO + t] = a_ref[0]
    abuf_ref[CONV_HALO + t:] = nxt
    nrows = t + 2 * CONV_HALO
    padded = abuf_ref[...]
    for r in range(1, SUBLANES):
        ash_ref[r - 1] = pltpu.roll(padded, nrows - r, axis=0)
    rows = 64
    base = CONV_HALO - CONV_PAD
    for r0 in range(0, t, rows):
        for c0 in range(0, CONV_CH, LANES):
            lanes = slice(c0, c0 + LANES)
            acc = jnp.broadcast_to(dwb_ref[:, lanes], (rows, LANES))
            for kk in range(CONV_K):
                al, r = divmod(base + kk, SUBLANES)
                lo = al * SUBLANES + r0
                src = abuf_ref if r == 0 else ash_ref.at[r - 1]
                acc = acc + dw_ref[kk:kk + 1, lanes] * src[lo:lo + rows, lanes]
            cbuf_ref[r0:r0 + rows, lanes] = acc
    c = cbuf_ref[...]
    mu = jnp.mean(c, axis=-1, keepdims=True)
    cc = c - mu
    var = jnp.mean(cc * cc, axis=-1, keepdims=True)
    y = cc * lax.rsqrt(var + EPS) * lng_ref[...] + lnb_ref[...]
    y = y * jax.nn.sigmoid(y)
    mixed = jnp.concatenate([y.astype(BF16), att_ref[0]], axis=-1)
    x1 = x_ref[0] + _rms(_dot(mixed, wout_ref[...]), gpost_ref[...])
    h = _rms(x1, gmem_ref[...]).astype(BF16)
    q = _dot(h, wq_ref[...]).astype(BF16)
    heads = []
    for hd in range(MEM_HEADS):
        sl = slice(hd * MEM_HEAD_DIM, (hd + 1) * MEM_HEAD_DIM)
        s = _dot(q[:, sl], kt_ref[0, sl, :])
        p = jnp.exp(s - jnp.max(s, axis=-1, keepdims=True))
        l = jnp.sum(p, axis=-1, keepdims=True)
        heads.append((_dot(p.astype(BF16), v_ref[0, :, sl]) / l).astype(BF16))
    o = jnp.concatenate(heads, axis=-1)
    o_ref[0] = x1 + _rms(_dot(o, wo_ref[...]), gmpost_ref[...])


def _post_attn(x, a, att, dw, dwb, lng, lnb, wout, gpost, gmem, wq, kt, v, wo, gmpost):
    b, s, d = x.shape
    t = TOK_TILE
    nt = s // t
    hb = t // CONV_HALO
    m = v.shape[1]
    vec = _resident((1, d))
    cvec = _resident((1, CONV_CH))
    return pl.pallas_call(
        _post_attn_kernel,
        grid=(b, nt),
        in_specs=[
            pl.BlockSpec((1, t, d), lambda i, j: (i, j, 0)),
            pl.BlockSpec((1, t, CONV_CH), lambda i, j: (i, j, 0)),
            pl.BlockSpec((1, CONV_HALO, CONV_CH),
                         lambda i, j: (i, jnp.maximum(j * hb - 1, 0), 0)),
            pl.BlockSpec((1, CONV_HALO, CONV_CH),
                         lambda i, j: (i, jnp.minimum((j + 1) * hb, nt * hb - 1), 0)),
            pl.BlockSpec((1, t, CONV_CH), lambda i, j: (i, j, 0)),
            _resident((CONV_K, CONV_CH)),
            cvec, cvec, cvec,
            _resident((d, d)),
            vec, vec,
            _resident((d, d)),
            pl.BlockSpec((1, d, m), lambda i, j: (i, 0, 0)),
            pl.BlockSpec((1, m, d), lambda i, j: (i, 0, 0)),
            _resident((d, d)),
            vec,
        ],
        out_specs=pl.BlockSpec((1, t, d), lambda i, j: (i, j, 0)),
        out_shape=jax.ShapeDtypeStruct((b, s, d), F32),
        scratch_shapes=[
            pltpu.VMEM((t + 2 * CONV_HALO, CONV_CH), F32),
            pltpu.VMEM((SUBLANES - 1, t + 2 * CONV_HALO, CONV_CH), F32),
            pltpu.VMEM((t, CONV_CH), F32),
        ],
        compiler_params=pltpu.CompilerParams(
            dimension_semantics=("arbitrary", "arbitrary"),
            vmem_limit_bytes=VMEM_LIMIT),
        name="post_attn",
    )(x, a, a, a, att, dw, dwb, lng, lnb, wout, gpost, gmem, wq, kt, v, wo, gmpost)


def _gelu_tanh(x):
    c = math.sqrt(2.0 / math.pi)
    half = 0.5 * x
    return half + half * jnp.tanh(x * (c + (c * 0.044715) * (x * x)))


def _ffn_kernel(x_ref, xp_ref, xn_ref, g_ref, wup_ref, dw_ref, dwb_ref, wdn_ref,
                gpost_ref, o_ref, hbuf_ref, hb_ref, act_ref, u_ref):
    t = x_ref.shape[1]
    j = pl.program_id(1)
    nt = pl.num_programs(1)
    g = g_ref[...]
    hbuf_ref[:FFN_HALO] = jnp.where(j > 0, _rms(xp_ref[0], g), 0.0)
    hbuf_ref[FFN_HALO:FFN_HALO + t] = _rms(x_ref[0], g)
    hbuf_ref[FFN_HALO + t:] = jnp.where(j < nt - 1, _rms(xn_ref[0], g), 0.0)
    hb_ref[...] = hbuf_ref[...].astype(BF16)
    rows = t + 2 * FFN_HALO
    d_ff = wdn_ref.shape[0]
    nc = d_ff // FFN_CHUNK

    def gate_val(ref, c):
        lo = c * FFN_CHUNK
        return jnp.concatenate([ref[:, lo:lo + FFN_CHUNK],
                                ref[:, d_ff + lo:d_ff + lo + FFN_CHUNK]], axis=1)

    def up(c, slot):
        u_ref[slot] = _dot(hb_ref[...], gate_val(wup_ref, c))

    def down(c, slot):
        u = u_ref[slot]
        dw = gate_val(dw_ref, c)
        prev = pltpu.roll(u, 1, axis=0)
        nxt = pltpu.roll(u, rows - 1, axis=0)
        mid = slice(FFN_HALO, FFN_HALO + t)
        conv = (dw[0:1] * prev[mid] + dw[1:2] * u[mid] + dw[2:3] * nxt[mid]
                + gate_val(dwb_ref, c))
        act = _gelu_tanh(conv[:, :FFN_CHUNK]) * conv[:, FFN_CHUNK:]
        act_ref[:, c * FFN_CHUNK:(c + 1) * FFN_CHUNK] = act.astype(BF16)

    up(0, 0)
    for c in range(nc):
        if c + 1 < nc:
            up(c + 1, (c + 1) % 2)
        down(c, c % 2)
    o_ref[0] = x_ref[0] + _rms(_dot(act_ref[...], wdn_ref[...]), gpost_ref[...])


def _ffn(x, g, wup, dw, dwb, wdn, gpost):
    b, s, d = x.shape
    t = FFN_TILE
    nt = s // t
    hb = t // FFN_HALO
    d_ff = wdn.shape[0]
    return pl.pallas_call(
        _ffn_kernel,
        grid=(b, nt),
        in_specs=[
            pl.BlockSpec((1, t, d), lambda i, j: (i, j, 0)),
            pl.BlockSpec((1, FFN_HALO, d),
                         lambda i, j: (i, jnp.maximum(j * hb - 1, 0), 0)),
            pl.BlockSpec((1, FFN_HALO, d),
                         lambda i, j: (i, jnp.minimum((j + 1) * hb, nt * hb - 1), 0)),
            _resident((1, d)),
            _resident((d, 2 * d_ff)),
            _resident((3, 2 * d_ff)),
            _resident((1, 2 * d_ff)),
            _resident((d_ff, d)),
            _resident((1, d)),
        ],
        out_specs=pl.BlockSpec((1, t, d), lambda i, j: (i, j, 0)),
        out_shape=jax.ShapeDtypeStruct((b, s, d), F32),
        scratch_shapes=[
            pltpu.VMEM((t + 2 * FFN_HALO, d), F32),
            pltpu.VMEM((t + 2 * FFN_HALO, d), BF16),
            pltpu.VMEM((t, d_ff), BF16),
            pltpu.VMEM((2, t + 2 * FFN_HALO, 2 * FFN_CHUNK), F32),
        ],
        compiler_params=pltpu.CompilerParams(
            dimension_semantics=("arbitrary", "arbitrary"),
            vmem_limit_bytes=VMEM_LIMIT),
        name="ffn",
    )(x, x, x, g, wup, dw, dwb, wdn, gpost)


def _rope_tables_t(s):
    f32 = np.float32
    rows = s // GRID_W
    r = np.repeat(np.arange(rows, dtype=f32), GRID_W)
    c = np.tile(np.arange(GRID_W, dtype=f32), rows)
    axis_dim = HEAD_DIM // 2
    inv = f32(ROPE_THETA) ** (-np.arange(0, axis_dim, 2, dtype=f32) / f32(axis_dim))
    ang_r = (r[:, None] * inv[None, :]).T.astype(f32)
    ang_c = (c[:, None] * inv[None, :]).T.astype(f32)
    cos_t = np.concatenate([np.cos(ang_r)] * 2 + [np.cos(ang_c)] * 2, axis=0)
    sr, sc = np.sin(ang_r), np.sin(ang_c)
    sin_t = np.concatenate([-sr, sr, -sc, sc], axis=0)
    return jnp.asarray(cos_t, F32), jnp.asarray(sin_t, F32)


def kernel(x, mem, norm_mix_pre, w_in, conv_dw, conv_dw_b, conv_ln_g, conv_ln_b,
           q_norm_g, k_norm_g, w_out, norm_mix_post, norm_mem_pre, mem_norm_g,
           w_mem_q, w_mem_kv, w_mem_o, norm_mem_post, norm_ffn_pre, w_up, ffn_dw,
           ffn_dw_b, w_down, norm_ffn_post):
    depth = w_in.shape[0]
    s = x.shape[1]
    assert s % TOK_TILE == 0 and s % ATT_TK == 0 and TOK_TILE % ATT_TK == 0
    assert w_down.shape[1] % FFN_CHUNK == 0
    cos_t, sin_t = _rope_tables_t(s)
    row = lambda v: v.reshape(1, -1)
    col = lambda v: jnp.broadcast_to(v[:, None], (v.shape[0], LANES))
    for l in range(depth):
        glu_cols = 2 * CONV_CH
        w_a = w_in[l, :, :glu_cols].astype(BF16)
        w_t = w_in[l, :, glu_cols:].T.astype(BF16)
        a, qt, qn, k, kn2, vt = _in_proj(x, row(norm_mix_pre[l]), w_a, w_t,
                                         col(q_norm_g[l]), col(k_norm_g[l]), cos_t, sin_t)
        att = _attention(qt, qn, k, kn2, vt)
        kt_mem, v_mem = _mem_kv(mem, row(mem_norm_g[l]), w_mem_kv[l].astype(BF16))
        x = _post_attn(x, a, att, conv_dw[l], row(conv_dw_b[l]), row(conv_ln_g[l]),
                       row(conv_ln_b[l]), w_out[l].astype(BF16), row(norm_mix_post[l]),
                       row(norm_mem_pre[l]), w_mem_q[l].astype(BF16), kt_mem, v_mem,
                       w_mem_o[l].astype(BF16), row(norm_mem_post[l]))
        x = _ffn(x, row(norm_ffn_pre[l]), w_up[l].astype(BF16), ffn_dw[l],
                 row(ffn_dw_b[l]), w_down[l].astype(BF16), row(norm_ffn_post[l]))
    return x
```

```python
import math

import jax
import jax.numpy as jnp
import numpy as np
from jax import lax
from jax.experimental import pallas as pl
from jax.experimental.pallas import tpu as pltpu

F32 = jnp.float32
BF16 = jnp.bfloat16

GRID_W = 64
HEAD_DIM = 64
ATT_HEADS = 8
ATT_KV_HEADS = 2
HEADS_PER_KV = ATT_HEADS // ATT_KV_HEADS
CONV_CH = 512
CONV_K = 31
CONV_PAD = (CONV_K - 1) // 2
MEM_HEADS = 4
MEM_HEAD_DIM = 256
ROPE_THETA = 10000.0
EPS = 1e-6
LOG2E = 1.4426950408889634

LANES = 128
SUBLANES = 8
BF16_ROWS = 16
VMEM_LIMIT = 56 * 1024 * 1024

TOK_TILE = 1024
FFN_TILE = 512
ATT_TQ = 256
ATT_N = HEADS_PER_KV * ATT_TQ
ATT_TK = 256
ATT_UNROLL = 16
ATT_FINISH_AT = 4
V_ROWS =HEAD_DIM + BF16_ROWS
CONV_HALO = 16
FFN_HALO = 8
FFN_CHUNK = 128


def _rms(xf, g):
    return xf * lax.rsqrt(jnp.mean(xf * xf, axis=-1, keepdims=True) + EPS) * g


def _dot(a, b):
    return jnp.dot(a, b, preferred_element_type=F32)


def _resident(shape):
    zeros = (0,) * len(shape)
    return pl.BlockSpec(shape, lambda *_: zeros, pipeline_mode=pl.Buffered(1))


def _mem_kv_kernel(mem_ref, g_ref, w_ref, kt_ref, v_ref):
    d = mem_ref.shape[-1]
    mn = _rms(mem_ref[0], g_ref[...]).astype(BF16)
    kv = _dot(mn, w_ref[...])
    kt_ref[0] = (kv[:, :d] * (MEM_HEAD_DIM ** -0.5)).T.astype(BF16)
    v_ref[0] = kv[:, d:].astype(BF16)


def _mem_kv(mem, g, w_kv):
    b, m, d = mem.shape
    return pl.pallas_call(
        _mem_kv_kernel,
        grid=(b,),
        in_specs=[
            pl.BlockSpec((1, m, d), lambda i: (i, 0, 0)),
            _resident((1, d)),
            _resident((d, 2 * d)),
        ],
        out_specs=[
            pl.BlockSpec((1, d, m), lambda i: (i, 0, 0)),
            pl.BlockSpec((1, m, d), lambda i: (i, 0, 0)),
        ],
        out_shape=[
            jax.ShapeDtypeStruct((b, d, m), BF16),
            jax.ShapeDtypeStruct((b, m, d), BF16),
        ],
        compiler_params=pltpu.CompilerParams(
            dimension_semantics=("arbitrary",), vmem_limit_bytes=VMEM_LIMIT),
        name="mem_kv",
    )(mem, g, w_kv)


def _rope_partner(t):
    q = HEAD_DIM // 4
    return jnp.concatenate([t[q:2 * q], t[:q], t[3 * q:], t[2 * q:3 * q]], axis=0)


def _norm_rope_t(zt, g, cos_t, sin_t):
    r = lax.rsqrt(jnp.mean(zt * zt, axis=0, keepdims=True) + EPS)
    y = zt * r * g
    return y * cos_t + _rope_partner(y) * sin_t


def _col_norm2(t_bf16):
    tf = t_bf16.astype(F32)
    return jnp.sum(tf * tf, axis=0, keepdims=True)


def _in_proj_kernel(x_ref, g_ref, wa_ref, wt_ref, gq_ref, gk_ref, cos_ref, sin_ref,
                    a_ref, qt_ref, qn_ref, k_ref, kn2_ref, vt_ref):
    t = x_ref.shape[1]
    h = _rms(x_ref[0], g_ref[...]).astype(BF16)
    za = _dot(h, wa_ref[...])
    a_ref[0] = za[:, :CONV_CH] * jax.nn.sigmoid(za[:, CONV_CH:])
    zt = lax.dot_general(wt_ref[...], h, (((1,), (1,)), ((), ())),
                         preferred_element_type=F32)
    reps = t // LANES
    gq = jnp.concatenate([gq_ref[...]] * reps, axis=1)
    gk = jnp.concatenate([gk_ref[...]] * reps, axis=1)
    cos_t = cos_ref[...]
    sin_t = sin_ref[...]
    qscale = (HEAD_DIM ** -0.5) * LOG2E
    zero = jnp.zeros((HEAD_DIM, ATT_TQ), BF16)
    for hd in range(ATT_HEADS):
        q = _norm_rope_t(zt[hd * HEAD_DIM:(hd + 1) * HEAD_DIM], gq, cos_t, sin_t)
        q = (q * qscale).astype(BF16)
        qn = jnp.sqrt(_col_norm2(q))
        kv, hh = divmod(hd, HEADS_PER_KV)
        cols = slice(hh * ATT_TQ, (hh + 1) * ATT_TQ)
        for c in range(t // ATT_TQ):
            toks = slice(c * ATT_TQ, (c + 1) * ATT_TQ)
            for j in range(ATT_KV_HEADS):
                qt_ref[0, kv, c, j * HEAD_DIM:(j + 1) * HEAD_DIM, cols] = (
                    q[:, toks] if j == kv else zero)
            qn_ref[0, kv, c, :, cols] = qn[:, toks]
    o = ATT_HEADS * HEAD_DIM
    ks = [_norm_rope_t(zt[o + j * HEAD_DIM:o + (j + 1) * HEAD_DIM], gk, cos_t, sin_t)
          for j in range(ATT_KV_HEADS)]
    for j in range(ATT_KV_HEADS):
        kn2_ref[0, j] = _col_norm2(ks[j].astype(BF16))
    k_ref[0] = jnp.concatenate(ks, axis=0).T.astype(BF16)
    o += ATT_KV_HEADS * HEAD_DIM
    row = lax.broadcasted_iota(jnp.int32, (BF16_ROWS, ATT_TK), 0)
    ones_rows = jnp.where(row == 0, 1.0, 0.0).astype(BF16)
    for j in range(ATT_KV_HEADS):
        v = zt[o + j * HEAD_DIM:o + (j + 1) * HEAD_DIM].astype(BF16)
        for c in range(t // ATT_TK):
            vt_ref[0, j, c, :HEAD_DIM, :] = v[:, c * ATT_TK:(c + 1) * ATT_TK]
            vt_ref[0, j, c, HEAD_DIM:, :] = ones_rows


def _in_proj(x, g, w_a, w_t, gq, gk, cos_t, sin_t):
    b, s, d = x.shape
    t = TOK_TILE
    nt = s // t
    qkv_rows = w_t.shape[0]
    return pl.pallas_call(
        _in_proj_kernel,
        grid=(b, nt),
        in_specs=[
            pl.BlockSpec((1, t, d), lambda i, j: (i, j, 0)),
            _resident((1, d)),
            _resident((d, 2 * CONV_CH)),
            _resident((qkv_rows, d)),
            _resident((HEAD_DIM, LANES)),
            _resident((HEAD_DIM, LANES)),
            pl.BlockSpec((HEAD_DIM, t), lambda i, j: (0, j)),
            pl.BlockSpec((HEAD_DIM, t), lambda i, j: (0, j)),
        ],
        out_specs=[
            pl.BlockSpec((1, t, CONV_CH), lambda i, j: (i, j, 0)),
            pl.BlockSpec((1, ATT_KV_HEADS, t // ATT_TQ, 2 * HEAD_DIM, ATT_N),
                         lambda i, j: (i, 0, j, 0, 0)),
            pl.BlockSpec((1, ATT_KV_HEADS, t // ATT_TQ, 1, ATT_N),
                         lambda i, j: (i, 0, j, 0, 0)),
            pl.BlockSpec((1, t, 2 * HEAD_DIM), lambda i, j: (i, j, 0)),
            pl.BlockSpec((1, ATT_KV_HEADS, 1, t), lambda i, j: (i, 0, 0, j)),
            pl.BlockSpec((1, ATT_KV_HEADS, t // ATT_TK, V_ROWS, ATT_TK),
                         lambda i, j: (i, 0, j, 0, 0)),
        ],
        out_shape=[
            jax.ShapeDtypeStruct((b, s, CONV_CH), F32),
            jax.ShapeDtypeStruct((b, ATT_KV_HEADS, s // ATT_TQ, 2 * HEAD_DIM, ATT_N), BF16),
            jax.ShapeDtypeStruct((b, ATT_KV_HEADS, s // ATT_TQ, 1, ATT_N), F32),
            jax.ShapeDtypeStruct((b, s, 2 * HEAD_DIM), BF16),
            jax.ShapeDtypeStruct((b, ATT_KV_HEADS, 1, s), F32),
            jax.ShapeDtypeStruct((b, ATT_KV_HEADS, s // ATT_TK, V_ROWS, ATT_TK), BF16),
        ],
        compiler_params=pltpu.CompilerParams(
            dimension_semantics=("arbitrary", "arbitrary"),
            vmem_limit_bytes=VMEM_LIMIT),
        name="in_proj",
    )(x, g, w_a, w_t, gq, gk, cos_t, sin_t)


ATT_MIN_DENOM = 2.0 ** -80


def _attn_kernel(qt_ref, qn_ref, k_ref, kn2_ref, vt_ref, o_ref, acc_ref, fin_ref, st_ref,
                 m_ref):
    nq = qt_ref.shape[2]
    nkb = vt_ref.shape[2]

    def k_block(kb):
        return k_ref[0, pl.ds(pl.multiple_of(kb * ATT_TK, ATT_TK), ATT_TK), :]

    kmax = jnp.sqrt(jnp.max(kn2_ref[0, 0], axis=-1, keepdims=True))

    def scores(qb, kb, slot):
        st_ref[slot] = _dot(k_block(kb), qt_ref[0, 0, qb])

    def accumulate(kb, slot, shift):
        p = jnp.exp2(st_ref[slot] - shift).astype(BF16)
        acc_ref[...] += _dot(vt_ref[0, 0, kb], p)

    def exact_block(qb):
        m_ref[...] = jnp.full(m_ref.shape, -jnp.inf, F32)
        acc_ref[...] = jnp.zeros(acc_ref.shape, F32)

        def exact(kb, carry):
            st = _dot(k_block(kb), qt_ref[0, 0, qb])
            m_old = m_ref[...]
            m_new = jnp.maximum(m_old, jnp.max(st, axis=0, keepdims=True))
            p = jnp.exp2(st - m_new).astype(BF16)
            acc_ref[...] = acc_ref[...] * jnp.exp2(m_old - m_new) + _dot(vt_ref[0, 0, kb], p)
            m_ref[...] = m_new
            return carry

        lax.fori_loop(0, nkb, exact, 0)

    scores(0, 0, 0)

    def finish(qb):
        acc = fin_ref[...]
        outs = [acc[:HEAD_DIM, h * ATT_TQ:(h + 1) * ATT_TQ]
                / acc[HEAD_DIM:HEAD_DIM + 1, h * ATT_TQ:(h + 1) * ATT_TQ]
                for h in range(HEADS_PER_KV)]
        rows = pl.ds(pl.multiple_of(qb * ATT_TQ, ATT_TQ), ATT_TQ)
        o_ref[0, rows, :] = jnp.concatenate(outs, axis=0).T.astype(BF16)

    fin_ref[...] = jnp.ones(fin_ref.shape, F32)

    def query_block(qb, carry):
        shift = qn_ref[0, 0, qb] * kmax
        acc_ref[...] = jnp.zeros(acc_ref.shape, F32)

        def group(i, c):
            for u in range(ATT_UNROLL):
                kb = ATT_UNROLL * i + u
                scores(qb, kb + 1, (u + 1) % 2)
                accumulate(kb, u % 2, shift)
            return c

        lax.fori_loop(0, nkb // ATT_UNROLL - 1, group, 0)
        for kb in range(nkb - ATT_UNROLL, nkb):
            if kb + 1 < nkb:
                scores(qb, kb + 1, (kb + 1) % 2)
            else:
                scores(jnp.minimum(qb + 1, nq - 1), 0, 0)
            accumulate(kb, kb % 2, shift)
            if kb == nkb - ATT_UNROLL + ATT_FINISH_AT:
                finish(jnp.maximum(qb - 1, 0))

        denom_ok = jnp.min(acc_ref[HEAD_DIM:HEAD_DIM + 1, :]) >= ATT_MIN_DENOM
        pl.when(jnp.logical_not(denom_ok))(lambda: exact_block(qb))
        fin_ref[...] = acc_ref[...]
        return carry

    lax.fori_loop(0, nq, query_block, 0)
    finish(nq - 1)


def _attention(qt, qn, k, kn2, vt):
    b, s, _ = k.shape
    nq = qt.shape[2]
    nkb = vt.shape[2]
    assert nkb % ATT_UNROLL == 0 and nkb >= 2 * ATT_UNROLL and ATT_UNROLL % 2 == 0
    width = HEADS_PER_KV * HEAD_DIM
    return pl.pallas_call(
        _attn_kernel,
        grid=(b, ATT_KV_HEADS),
        in_specs=[
            pl.BlockSpec((1, 1, nq, 2 * HEAD_DIM, ATT_N), lambda i, j: (i, j, 0, 0, 0)),
            pl.BlockSpec((1, 1, nq, 1, ATT_N), lambda i, j: (i, j, 0, 0, 0)),
            pl.BlockSpec((1, s, 2 * HEAD_DIM), lambda i, j: (i, 0, 0)),
            pl.BlockSpec((1, 1, 1, s), lambda i, j: (i, j, 0, 0)),
            pl.BlockSpec((1, 1, nkb, V_ROWS, ATT_TK), lambda i, j: (i, j, 0, 0, 0)),
        ],
        out_specs=pl.BlockSpec((1, s, width), lambda i, j: (i, 0, j)),
        out_shape=jax.ShapeDtypeStruct((b, s, ATT_KV_HEADS * width), BF16),
        scratch_shapes=[
            pltpu.VMEM((V_ROWS, ATT_N), F32),
            pltpu.VMEM((V_ROWS, ATT_N), F32),
            pltpu.VMEM((2, ATT_TK, ATT_N), F32),
            pltpu.VMEM((1, ATT_N), F32),
        ],
        compiler_params=pltpu.CompilerParams(
            dimension_semantics=("arbitrary", "arbitrary"),
            vmem_limit_bytes=VMEM_LIMIT),
        name="attention",
    )(qt, qn, k, kn2, vt)


def _post_attn_kernel(x_ref, a_ref, ap_ref, an_ref, att_ref, dw_ref, dwb_ref,
                      lng_ref, lnb_ref, wout_ref, gpost_ref, gmem_ref, wq_ref,
                      kt_ref, v_ref, wo_ref, gmpost_ref, o_ref, abuf_ref, ash_ref, cbuf_ref):
    t = x_ref.shape[1]
    j = pl.program_id(1)
    nt = pl.num_programs(1)
    prev = jnp.where(j > 0, ap_ref[0], 0.0)
    nxt = jnp.where(j < nt - 1, an_ref[0], 0.0)
    abuf_ref[:CONV_HALO] = prev
    abuf_ref[CONV_HALO:CONV_HALO + t] = a_ref[0]
    abuf_ref[CONV_HALO + t:] = nxt
    nrows = t + 2 * CONV_HALO
    padded = abuf_ref[...]
    for r in range(1, SUBLANES):
        ash_ref[r - 1] = pltpu.roll(padded, nrows - r, axis=0)
    rows = 64
    base = CONV_HALO - CONV_PAD
    for r0 in range(0, t, rows):
        for c0 in range(0, CONV_CH, LANES):
            lanes = slice(c0, c0 + LANES)
            acc = jnp.broadcast_to(dwb_ref[:, lanes], (rows, LANES))
            for kk in range(CONV_K):
                al, r = divmod(base + kk, SUBLANES)
                lo = al * SUBLANES + r0
                src = abuf_ref if r == 0 else ash_ref.at[r - 1]
                acc = acc + dw_ref[kk:kk + 1, lanes] * src[lo:lo + rows, lanes]
            cbuf_ref[r0:r0 + rows, lanes] = acc
    c = cbuf_ref[...]
    mu = jnp.mean(c, axis=-1, keepdims=True)
    cc = c - mu
    var = jnp.mean(cc * cc, axis=-1, keepdims=True)
    y = cc * lax.rsqrt(var + EPS) * lng_ref[...] + lnb_ref[...]
    y = y * jax.nn.sigmoid(y)
    mixed = jnp.concatenate([y.astype(BF16), att_ref[0]], axis=-1)
    x1 = x_ref[0] + _rms(_dot(mixed, wout_ref[...]), gpost_ref[...])
    h = _rms(x1, gmem_ref[...]).astype(BF16)
    q = _dot(h, wq_ref[...]).astype(BF16)
    heads = []
    for hd in range(MEM_HEADS):
        sl = slice(hd * MEM_HEAD_DIM, (hd + 1) * MEM_HEAD_DIM)
        s = _dot(q[:, sl], kt_ref[0, sl, :])
        p = jnp.exp(s - jnp.max(s, axis=-1, keepdims=True))
        l = jnp.sum(p, axis=-1, keepdims=True)
        heads.append((_dot(p.astype(BF16), v_ref[0, :, sl]) / l).astype(BF16))
    o = jnp.concatenate(heads, axis=-1)
    o_ref[0] = x1 + _rms(_dot(o, wo_ref[...]), gmpost_ref[...])


def _post_attn(x, a, att, dw, dwb, lng, lnb, wout, gpost, gmem, wq, kt, v, wo, gmpost):
    b, s, d = x.shape
    t = TOK_TILE
    nt = s // t
    hb = t // CONV_HALO
    m = v.shape[1]
    vec = _resident((1, d))
    cvec = _resident((1, CONV_CH))
    return pl.pallas_call(
        _post_attn_kernel,
        grid=(b, nt),
        in_specs=[
            pl.BlockSpec((1, t, d), lambda i, j: (i, j, 0)),
            pl.BlockSpec((1, t, CONV_CH), lambda i, j: (i, j, 0)),
            pl.BlockSpec((1, CONV_HALO, CONV_CH),
                         lambda i, j: (i, jnp.maximum(j * hb - 1, 0), 0)),
            pl.BlockSpec((1, CONV_HALO, CONV_CH),
                         lambda i, j: (i, jnp.minimum((j + 1) * hb, nt * hb - 1), 0)),
            pl.BlockSpec((1, t, CONV_CH), lambda i, j: (i, j, 0)),
            _resident((CONV_K, CONV_CH)),
            cvec, cvec, cvec,
            _resident((d, d)),
            vec, vec,
            _resident((d, d)),
            pl.BlockSpec((1, d, m), lambda i, j: (i, 0, 0)),
            pl.BlockSpec((1, m, d), lambda i, j: (i, 0, 0)),
            _resident((d, d)),
            vec,
        ],
        out_specs=pl.BlockSpec((1, t, d), lambda i, j: (i, j, 0)),
        out_shape=jax.ShapeDtypeStruct((b, s, d), F32),
        scratch_shapes=[
            pltpu.VMEM((t + 2 * CONV_HALO, CONV_CH), F32),
            pltpu.VMEM((SUBLANES - 1, t + 2 * CONV_HALO, CONV_CH), F32),
            pltpu.VMEM((t, CONV_CH), F32),
        ],
        compiler_params=pltpu.CompilerParams(
            dimension_semantics=("arbitrary", "arbitrary"),
            vmem_limit_bytes=VMEM_LIMIT),
        name="post_attn",
    )(x, a, a, a, att, dw, dwb, lng, lnb, wout, gpost, gmem, wq, kt, v, wo, gmpost)


def _gelu_tanh(x):
    c = math.sqrt(2.0 / math.pi)
    half = 0.5 * x
    return half + half * jnp.tanh(x * (c + (c * 0.044715) * (x * x)))


def _ffn_kernel(x_ref, xp_ref, xn_ref, g_ref, wup_ref, dw_ref, dwb_ref, wdn_ref,
                gpost_ref, o_ref, hbuf_ref, hb_ref, act_ref, u_ref):
    t = x_ref.shape[1]
    j = pl.program_id(1)
    nt = pl.num_programs(1)
    g = g_ref[...]
    hbuf_ref[:FFN_HALO] = jnp.where(j > 0, _rms(xp_ref[0], g), 0.0)
    hbuf_ref[FFN_HALO:FFN_HALO + t] = _rms(x_ref[0], g)
    hbuf_ref[FFN_HALO + t:] = jnp.where(j < nt - 1, _rms(xn_ref[0], g), 0.0)
    hb_ref[...] = hbuf_ref[...].astype(BF16)
    rows = t + 2 * FFN_HALO
    d_ff = wdn_ref.shape[0]
    nc = d_ff // FFN_CHUNK

    def gate_val(ref, c):
        lo = c * FFN_CHUNK
        return jnp.concatenate([ref[:, lo:lo + FFN_CHUNK],
                                ref[:, d_ff + lo:d_ff + lo + FFN_CHUNK]], axis=1)

    def up(c, slot):
        u_ref[slot] = _dot(hb_ref[...], gate_val(wup_ref, c))

    def down(c, slot):
        u = u_ref[slot]
        dw = gate_val(dw_ref, c)
        prev = pltpu.roll(u, 1, axis=0)
        nxt = pltpu.roll(u, rows - 1, axis=0)
        mid = slice(FFN_HALO, FFN_HALO + t)
        conv = (dw[0:1] * prev[mid] + dw[1:2] * u[mid] + dw[2:3] * nxt[mid]
                + gate_val(dwb_ref, c))
        act = _gelu_tanh(conv[:, :FFN_CHUNK]) * conv[:, FFN_CHUNK:]
        act_ref[:, c * FFN_CHUNK:(c + 1) * FFN_CHUNK] = act.astype(BF16)

    up(0, 0)
    for c in range(nc):
        if c + 1 < nc:
            up(c + 1, (c + 1) % 2)
        down(c, c % 2)
    o_ref[0] = x_ref[0] + _rms(_dot(act_ref[...], wdn_ref[...]), gpost_ref[...])


def _ffn(x, g, wup, dw, dwb, wdn, gpost):
    b, s, d = x.shape
    t = FFN_TILE
    nt = s // t
    hb = t // FFN_HALO
    d_ff = wdn.shape[0]
    return pl.pallas_call(
        _ffn_kernel,
        grid=(b, nt),
        in_specs=[
            pl.BlockSpec((1, t, d), lambda i, j: (i, j, 0)),
            pl.BlockSpec((1, FFN_HALO, d),
                         lambda i, j: (i, jnp.maximum(j * hb - 1, 0), 0)),
            pl.BlockSpec((1, FFN_HALO, d),
                         lambda i, j: (i, jnp.minimum((j + 1) * hb, nt * hb - 1), 0)),
            _resident((1, d)),
            _resident((d, 2 * d_ff)),
            _resident((3, 2 * d_ff)),
            _resident((1, 2 * d_ff)),
            _resident((d_ff, d)),
            _resident((1, d)),
        ],
        out_specs=pl.BlockSpec((1, t, d), lambda i, j: (i, j, 0)),
        out_shape=jax.ShapeDtypeStruct((b, s, d), F32),
        scratch_shapes=[
            pltpu.VMEM((t + 2 * FFN_HALO, d), F32),
            pltpu.VMEM((t + 2 * FFN_HALO, d), BF16),
            pltpu.VMEM((t, d_ff), BF16),
            pltpu.VMEM((2, t + 2 * FFN_HALO, 2 * FFN_CHUNK), F32),
        ],
        compiler_params=pltpu.CompilerParams(
            dimension_semantics=("arbitrary", "arbitrary"),
            vmem_limit_bytes=VMEM_LIMIT),
        name="ffn",
    )(x, x, x, g, wup, dw, dwb, wdn, gpost)


def _rope_tables_t(s):
    f32 = np.float32
    rows = s // GRID_W
    r = np.repeat(np.arange(rows, dtype=f32), GRID_W)
    c = np.tile(np.arange(GRID_W, dtype=f32), rows)
    axis_dim = HEAD_DIM // 2
    inv = f32(ROPE_THETA) ** (-np.arange(0, axis_dim, 2, dtype=f32) / f32(axis_dim))
    ang_r = (r[:, None] * inv[None, :]).T.astype(f32)
    ang_c = (c[:, None] * inv[None, :]).T.astype(f32)
    cos_t = np.concatenate([np.cos(ang_r)] * 2 + [np.cos(ang_c)] * 2, axis=0)
    sr, sc = np.sin(ang_r), np.sin(ang_c)
    sin_t = np.concatenate([-sr, sr, -sc, sc], axis=0)
    return jnp.asarray(cos_t, F32), jnp.asarray(sin_t, F32)


def kernel(x, mem, norm_mix_pre, w_in, conv_dw, conv_dw_b, conv_ln_g, conv_ln_b,
           q_norm_g, k_norm_g, w_out, norm_mix_post, norm_mem_pre, mem_norm_g,
           w_mem_q, w_mem_kv, w_mem_o, norm_mem_post, norm_ffn_pre, w_up, ffn_dw,
           ffn_dw_b, w_down, norm_ffn_post):
    depth = w_in.shape[0]
    s = x.shape[1]
    assert s % TOK_TILE == 0 and s % ATT_TK == 0 and TOK_TILE % ATT_TK == 0
    assert w_down.shape[1] % FFN_CHUNK == 0
    cos_t, sin_t = _rope_tables_t(s)
    row = lambda v: v.reshape(1, -1)
    col = lambda v: jnp.broadcast_to(v[:, None], (v.shape[0], LANES))
    for l in range(depth):
        glu_cols = 2 * CONV_CH
        w_a = w_in[l, :, :glu_cols].astype(BF16)
        w_t = w_in[l, :, glu_cols:].T.astype(BF16)
        a, qt, qn, k, kn2, vt = _in_proj(x, row(norm_mix_pre[l]), w_a, w_t,
                                         col(q_norm_g[l]), col(k_norm_g[l]), cos_t, sin_t)
        att = _attention(qt, qn, k, kn2, vt)
        kt_mem, v_mem = _mem_kv(mem, row(mem_norm_g[l]), w_mem_kv[l].astype(BF16))
        x = _post_attn(x, a, att, conv_dw[l], row(conv_dw_b[l]), row(conv_ln_g[l]),
                       row(conv_ln_b[l]), w_out[l].astype(BF16), row(norm_mix_post[l]),
                       row(norm_mem_pre[l]), w_mem_q[l].astype(BF16), kt_mem, v_mem,
                       w_mem_o[l].astype(BF16), row(norm_mem_post[l]))
        x = _ffn(x, row(norm_ffn_pre[l]), w_up[l].astype(BF16), ffn_dw[l],
                 row(ffn_dw_b[l]), w_down[l].astype(BF16), row(norm_ffn_post[l]))
    return x
```
